```python
import math
import jax, jax.numpy as jnp
from jax import lax
import numpy as np

D_MODEL = 1024
BATCH = 2
SEQ = 8192
DEPTH = 2

D_MIX = D_MODEL
HEAD_DIM = 64
CONV_CH = D_MODEL // 4
MOBA_HEADS = (3 * D_MODEL // 8) // HEAD_DIM
MOBA_W = MOBA_HEADS * HEAD_DIM
GDN_W = D_MIX - CONV_CH - MOBA_W
GDN_HEADS = GDN_W // HEAD_DIM
CONV_K = 31
MOBA_BLOCK = 256
MOBA_TOPK = 3
Q_BLOCK = 64
ROPE_THETA = 500000.0
ROPE_DIM = HEAD_DIM // 4
GDN_CONV_K = 4
GDN_CHUNK = 64
D_FF = ((8 * D_MODEL // 3 + 255) // 256) * 256
N_MOD = 9
EPS = 1e-6
IN_CONV = 2 * CONV_CH
IN_MOBA = 3 * MOBA_W
IN_GDN = 4 * GDN_W + 2 * GDN_HEADS
D_IN = IN_CONV + IN_MOBA + IN_GDN

kernel_name = "hymba_conformer_moba_gdn_trunk"

F32 = jnp.float32


def rmsnorm(x, g):
    xf = x.astype(F32)
    y = xf * lax.rsqrt(jnp.mean(xf * xf, axis=-1, keepdims=True) + EPS)
    return (y * g.astype(F32)).astype(x.dtype)


def modulate(x, g, shift, scale):
    return rmsnorm(x, g) * (1 + scale[:, None, :]) + shift[:, None, :]


def swiglu(h, w_gate, w_up, w_down):
    return (jax.nn.silu(h @ w_gate) * (h @ w_up)) @ w_down


def causal_depthwise_conv(x, w):
    K, C = w.shape
    xp = jnp.pad(x, ((0, 0), (K - 1, 0), (0, 0)))
    return lax.conv_general_dilated(xp, w[:, None, :].astype(x.dtype), window_strides=(1,), padding='VALID',
                                    dimension_numbers=('NWC', 'WIO', 'NWC'), feature_group_count=C)


def conformer_conv(u, w_dw, b_dw, ln_g, ln_b):
    a, gate = jnp.split(u, 2, axis=-1)
    h = a * jax.nn.sigmoid(gate)
    h = (causal_depthwise_conv(h, w_dw) + b_dw).astype(F32)
    mu = jnp.mean(h, axis=-1, keepdims=True)
    var = jnp.mean(jnp.square(h - mu), axis=-1, keepdims=True)
    h = (h - mu) * lax.rsqrt(var + EPS) * ln_g.astype(F32) + ln_b.astype(F32)
    return jax.nn.silu(h).astype(u.dtype)


def partial_rope(x, pos):
    half = ROPE_DIM // 2
    inv = jnp.exp(-math.log(ROPE_THETA) * jnp.arange(0, ROPE_DIM, 2, dtype=F32) / ROPE_DIM)
    ang = pos.astype(F32)[:, None] * inv[None, :]
    cos = jnp.cos(ang)[None, :, None, :]
    sin = jnp.sin(ang)[None, :, None, :]
    xr = x[..., :ROPE_DIM].astype(F32)
    x1, x2 = xr[..., :half], xr[..., half:]
    rot = jnp.concatenate([x1 * cos - x2 * sin, x2 * cos + x1 * sin], axis=-1)
    return jnp.concatenate([rot.astype(x.dtype), x[..., ROPE_DIM:]], axis=-1)


def moba_attention(q, k, v):
    B, S, H, dh = q.shape
    nb = -(-S // MOBA_BLOCK)
    s_pad = nb * MOBA_BLOCK
    q = q.transpose(0, 2, 1, 3)
    pad = ((0, 0), (0, 0), (0, s_pad - S), (0, 0))
    kb = jnp.pad(k.transpose(0, 2, 1, 3), pad).reshape(B, H, nb, MOBA_BLOCK, dh)
    vb = jnp.pad(v.transpose(0, 2, 1, 3), pad).reshape(B, H, nb, MOBA_BLOCK, dh)
    k_mean = jnp.mean(kb.astype(F32), axis=3)
    gate_s = jnp.einsum('bhsd,bhnd->bhsn', q.astype(F32), k_mean)
    q_blk = jnp.arange(S) // MOBA_BLOCK
    past = jnp.arange(nb)[None, :] < q_blk[:, None]
    gate_s = jnp.where(past, gate_s, -jnp.inf)
    topk = min(MOBA_TOPK, nb)
    _, sel = lax.top_k(gate_s, topk)
    sel_valid = sel < q_blk[:, None]
    scale = dh ** -0.5
    bidx = jnp.arange(B)[:, None, None, None]
    hidx = jnp.arange(H)[None, :, None, None]

    def one_block(i):
        start = i * Q_BLOCK
        qi = lax.dynamic_slice_in_dim(q, start, Q_BLOCK, axis=2)
        seli = lax.dynamic_slice_in_dim(sel, start, Q_BLOCK, axis=2)
        vali = lax.dynamic_slice_in_dim(sel_valid, start, Q_BLOCK, axis=2)
        ksel = kb[bidx, hidx, seli]
        vsel = vb[bidx, hidx, seli]
        own = start // MOBA_BLOCK
        kown = lax.dynamic_index_in_dim(kb, own, axis=2, keepdims=False)
        vown = lax.dynamic_index_in_dim(vb, own, axis=2, keepdims=False)
        s_sel = jnp.einsum('bhqd,bhqnkd->bhqnk', qi, ksel, preferred_element_type=F32) * scale
        s_sel = jnp.where(vali[..., None], s_sel, -jnp.inf).reshape(B, H, Q_BLOCK, topk * MOBA_BLOCK)
        s_own = jnp.einsum('bhqd,bhkd->bhqk', qi, kown, preferred_element_type=F32) * scale
        qpos = start + jnp.arange(Q_BLOCK)
        kpos = own * MOBA_BLOCK + jnp.arange(MOBA_BLOCK)
        s_own = jnp.where(kpos[None, :] <= qpos[:, None], s_own, -jnp.inf)
        p = jax.nn.softmax(jnp.concatenate([s_sel, s_own], axis=-1), axis=-1)
        p_sel = p[..., :topk * MOBA_BLOCK].reshape(B, H, Q_BLOCK, topk, MOBA_BLOCK)
        p_own = p[..., topk * MOBA_BLOCK:]
        o = (jnp.einsum('bhqnk,bhqnkd->bhqd', p_sel, vsel.astype(F32))
             + jnp.einsum('bhqk,bhkd->bhqd', p_own, vown.astype(F32)))
        return o.astype(q.dtype)

    out = lax.map(one_block, jnp.arange(S // Q_BLOCK))
    return out.transpose(1, 0, 3, 2, 4).reshape(B, S, H * dh)


def l2norm(x):
    return x * lax.rsqrt(jnp.sum(x * x, axis=-1, keepdims=True) + EPS)


def gated_delta_net(q, k, v, g, beta):
    B, S, H, dk = q.shape
    dv = v.shape[-1]
    C = GDN_CHUNK
    N = S // C
    q = q * dk ** -0.5
    tr = lambda t: t.transpose(0, 2, 1, 3).reshape(B, H, N, C, t.shape[-1])
    q, k, v = tr(q), tr(k), tr(v)
    g = jnp.cumsum(g.transpose(0, 2, 1).reshape(B, H, N, C), axis=-1)
    beta = beta.transpose(0, 2, 1).reshape(B, H, N, C)
    k_beta = k * beta[..., None]
    v_beta = v * beta[..., None]
    tril = jnp.tril(jnp.ones((C, C), dtype=bool))
    strict = jnp.tril(jnp.ones((C, C), dtype=bool), -1)
    decay = jnp.exp(jnp.where(tril, g[..., :, None] - g[..., None, :], -jnp.inf))
    L = jnp.where(strict, jnp.einsum('bhncd,bhnsd->bhncs', k_beta, k) * decay, 0.0)
    eye = jnp.eye(C, dtype=F32)
    T = lax.linalg.triangular_solve(eye + L, jnp.broadcast_to(eye, L.shape), left_side=True, lower=True)
    u = T @ v_beta
    w = T @ (k_beta * jnp.exp(g)[..., None])
    qk = jnp.where(tril, jnp.einsum('bhncd,bhnsd->bhncs', q, k) * decay, 0.0)
    g_last = g[..., -1]
    k_dec = k * jnp.exp(g_last[..., None] - g)[..., None]
    q_dec = q * jnp.exp(g)[..., None]

    def step(state, xs):
        q_i, qk_i, k_i, u_i, w_i, gl_i = xs
        v_new = u_i - w_i @ state
        o = q_i @ state + qk_i @ v_new
        state = state * jnp.exp(gl_i)[..., None, None] + jnp.swapaxes(k_i, -1, -2) @ v_new
        return state, o

    mv = lambda t: jnp.moveaxis(t, 2, 0)
    xs = (mv(q_dec), mv(qk), mv(k_dec), mv(u), mv(w), mv(g_last))
    state0 = jnp.zeros((B, H, dk, dv), F32)
    _, o = lax.scan(step, state0, xs)
    return o.transpose(1, 0, 3, 2, 4).reshape(B, S, H, dv)


def token_mixing(h, w_in, conv_w, conv_b, conv_ln_g, conv_ln_b, gdn_conv_w, gdn_a_log, gdn_dt_bias, gdn_norm_g, w_out):
    B, S, _ = h.shape
    u = h @ w_in
    u_conv, u_moba, u_gdn = jnp.split(u, [IN_CONV, IN_CONV + IN_MOBA], axis=-1)
    y_conv = conformer_conv(u_conv, conv_w, conv_b, conv_ln_g, conv_ln_b)
    mq, mk, mv = [t.reshape(B, S, MOBA_HEADS, HEAD_DIM) for t in jnp.split(u_moba, 3, axis=-1)]
    pos = jnp.arange(S)
    y_moba = moba_attention(partial_rope(mq, pos), partial_rope(mk, pos), mv)
    qkv = u_gdn[..., :3 * GDN_W]
    z = u_gdn[..., 3 * GDN_W:4 * GDN_W].reshape(B, S, GDN_HEADS, HEAD_DIM).astype(F32)
    a = u_gdn[..., 4 * GDN_W:4 * GDN_W + GDN_HEADS].astype(F32)
    b = u_gdn[..., 4 * GDN_W + GDN_HEADS:].astype(F32)
    qkv = jax.nn.silu(causal_depthwise_conv(qkv, gdn_conv_w)).astype(F32)
    gq, gk, gv = [t.reshape(B, S, GDN_HEADS, HEAD_DIM) for t in jnp.split(qkv, 3, axis=-1)]
    beta = jax.nn.sigmoid(b)
    g = -jnp.exp(gdn_a_log.astype(F32)) * jax.nn.softplus(a + gdn_dt_bias.astype(F32))
    o = gated_delta_net(l2norm(gq), l2norm(gk), gv, g, beta)
    o = rmsnorm(o, gdn_norm_g) * jax.nn.silu(z)
    y_gdn = o.reshape(B, S, GDN_W).astype(h.dtype)
    y = jnp.concatenate([y_conv, y_moba, y_gdn], axis=-1)
    return y @ w_out


def setup_inputs(seed: int = 0) -> dict:
    key = jax.random.key(seed)
    ks = jax.random.split(key, 24)
    nrm = lambda k, shape, s: jax.random.normal(k, shape, F32) * s
    gain = lambda k, shape: 1.0 + 0.02 * jax.random.normal(k, shape, F32)
    dt = jnp.exp(jax.random.uniform(ks[16], (DEPTH, GDN_HEADS), F32) * (math.log(0.1) - math.log(0.001)) + math.log(0.001))
    return {
        "x": jax.random.normal(ks[0], (BATCH, SEQ, D_MODEL), F32),
        "c": jax.random.normal(ks[1], (BATCH, D_MODEL), F32),
        "w_ada": nrm(ks[2], (DEPTH, D_MODEL, N_MOD * D_MODEL), 0.5 * D_MODEL ** -0.5),
        "b_ada": nrm(ks[3], (DEPTH, N_MOD * D_MODEL), 0.02),
        "ln_ffn1_g": gain(ks[4], (DEPTH, D_MODEL)),
        "ffn1_w_gate": nrm(ks[5], (DEPTH, D_MODEL, D_FF), D_MODEL ** -0.5),
        "ffn1_w_up": nrm(ks[6], (DEPTH, D_MODEL, D_FF), D_MODEL ** -0.5),
        "ffn1_w_down": nrm(ks[7], (DEPTH, D_FF, D_MODEL), D_FF ** -0.5),
        "ln_mix_g": gain(ks[8], (DEPTH, D_MODEL)),
        "w_in": nrm(ks[9], (DEPTH, D_MODEL, D_IN), D_MODEL ** -0.5),
        "conv_w": nrm(ks[10], (DEPTH, CONV_K, CONV_CH), CONV_K ** -0.5),
        "conv_b": nrm(ks[11], (DEPTH, CONV_CH), 0.02),
        "conv_ln_g": gain(ks[12], (DEPTH, CONV_CH)),
        "conv_ln_b": nrm(ks[13], (DEPTH, CONV_CH), 0.02),
        "gdn_conv_w": nrm(ks[14], (DEPTH, GDN_CONV_K, 3 * GDN_W), GDN_CONV_K ** -0.5),
        "gdn_a_log": jnp.log(jax.random.uniform(ks[15], (DEPTH, GDN_HEADS), F32, 1.0, 16.0)),
        "gdn_dt_bias": dt + jnp.log(-jnp.expm1(-dt)),
        "gdn_norm_g": gain(ks[17], (DEPTH, HEAD_DIM)),
        "w_out": nrm(ks[18], (DEPTH, D_MIX, D_MODEL), D_MIX ** -0.5),
        "ln_ffn2_g": gain(ks[19], (DEPTH, D_MODEL)),
        "ffn2_w_gate": nrm(ks[20], (DEPTH, D_MODEL, D_FF), D_MODEL ** -0.5),
        "ffn2_w_up": nrm(ks[21], (DEPTH, D_MODEL, D_FF), D_MODEL ** -0.5),
        "ffn2_w_down": nrm(ks[22], (DEPTH, D_FF, D_MODEL), D_FF ** -0.5),
        "final_g": gain(ks[23], (D_MODEL,)),
    }


def reference(x, c, w_ada, b_ada, ln_ffn1_g, ffn1_w_gate, ffn1_w_up, ffn1_w_down, ln_mix_g, w_in, conv_w, conv_b,
              conv_ln_g, conv_ln_b, gdn_conv_w, gdn_a_log, gdn_dt_bias, gdn_norm_g, w_out, ln_ffn2_g,
              ffn2_w_gate, ffn2_w_up, ffn2_w_down, final_g):
    c_act = jax.nn.silu(c)
    for l in range(DEPTH):
        mod = c_act @ w_ada[l] + b_ada[l]
        sh1, sc1, gt1, sh2, sc2, gt2, sh3, sc3, gt3 = jnp.split(mod, N_MOD, axis=-1)
        h = modulate(x, ln_ffn1_g[l], sh1, sc1)
        x = x + 0.5 * gt1[:, None, :] * swiglu(h, ffn1_w_gate[l], ffn1_w_up[l], ffn1_w_down[l])
        h = modulate(x, ln_mix_g[l], sh2, sc2)
        x = x + gt2[:, None, :] * token_mixing(h, w_in[l], conv_w[l], conv_b[l], conv_ln_g[l], conv_ln_b[l],
                                                gdn_conv_w[l], gdn_a_log[l], gdn_dt_bias[l], gdn_norm_g[l], w_out[l])
        h = modulate(x, ln_ffn2_g[l], sh3, sc3)
        x = x + 0.5 * gt3[:, None, :] * swiglu(h, ffn2_w_gate[l], ffn2_w_up[l], ffn2_w_down[l])
    return rmsnorm(x, final_g)
```

```python
import functools
import math

import jax
import jax.numpy as jnp
from jax import lax
from jax.experimental import pallas as pl
from jax.experimental.pallas import tpu as pltpu

F32 = jnp.float32
BF16 = jnp.bfloat16
HIGHEST = lax.Precision.HIGHEST

EPS = 1e-6
HEAD_DIM = 64
LANES = 128
N_MOD = 9
CONV_K = 31
CONV_HALO = 32
MOBA_BLOCK = 256
MOBA_TOPK = 3
ROPE_DIM = HEAD_DIM // 4
ROPE_THETA = 500000.0
GDN_CONV_K = 4
GDN_HALO = 8
GDN_CHUNK = 64
MASK_NEG = -(2.0 ** 100)
VMEM_LIMIT = 56 * 1024 * 1024


def _cparams(sem):
    return pltpu.CompilerParams(dimension_semantics=sem, vmem_limit_bytes=VMEM_LIMIT)


def _iota(shape, dim):
    return lax.broadcasted_iota(jnp.int32, shape, dim)


def _div64(t):
    return t >> 6


def _dot(a, b):
    return jnp.dot(a, b, preferred_element_type=F32)


def _dot_hi(a, b):
    return jnp.dot(a, b, preferred_element_type=F32, precision=HIGHEST)


def _dot_nt(a, b, precision=None):
    return lax.dot_general(a, b, (((1,), (1,)), ((), ())), preferred_element_type=F32, precision=precision)


def _silu(x):
    return x * jax.nn.sigmoid(x)


def _modulate(x, g, shift, scale):
    ms = jnp.mean(x * x, axis=-1, keepdims=True)
    y = x * lax.rsqrt(ms + EPS) * g
    return y * (1.0 + scale) + shift


def _ada_kernel(c_ref, w_ref, b_ref, o_ref):
    ca = _silu(c_ref[...]).astype(BF16)
    o_ref[0] = _dot(ca, w_ref[0].astype(BF16)) + b_ref[0]


def _ada_mod(c, w_ada, b_ada):
    depth, d, nd = w_ada.shape
    b = c.shape[0]
    rows = 8
    c8 = jnp.zeros((rows, d), F32).at[:b].set(c)
    out = pl.pallas_call(
        _ada_kernel,
        grid=(depth, nd // d),
        in_specs=[pl.BlockSpec((rows, d), lambda l, j: (0, 0)),
                  pl.BlockSpec((1, d, d), lambda l, j: (l, 0, j)),
                  pl.BlockSpec((1, 1, d), lambda l, j: (l, 0, j))],
        out_specs=pl.BlockSpec((1, rows, d), lambda l, j: (l, 0, j)),
        out_shape=jax.ShapeDtypeStruct((depth, rows, nd), F32),
        name="ada_mod",
        compiler_params=_cparams(("arbitrary", "arbitrary")),
    )(c8, w_ada, b_ada.reshape(depth, 1, nd))
    return out[:, :b].reshape(depth, b, nd // d, d)


def _ffn_kernel(x_ref, mod_ref, g_ref, wg_ref, wu_ref, wd_ref, fg_ref, o_ref, *, mod_base, final):
    x = x_ref[0]
    m = mod_ref[0]
    shift, scale, gate = m[mod_base:mod_base + 1], m[mod_base + 1:mod_base + 2], m[mod_base + 2:mod_base + 3]
    h = _modulate(x, g_ref[...], shift, scale).astype(BF16)
    a = _dot(h, wg_ref[...])
    u = _dot(h, wu_ref[...])
    act = (_silu(a) * u).astype(BF16)
    y = x + (0.5 * gate) * _dot(act, wd_ref[...])
    if final:
        y = y * lax.rsqrt(jnp.mean(y * y, axis=-1, keepdims=True) + EPS) * fg_ref[...]
    o_ref[0] = y


def _ffn(x, mod, g, wg, wu, wd, final_g, *, mod_base, final, tm):
    b, s, d = x.shape
    f = wg.shape[1]
    resident = dict(pipeline_mode=pl.Buffered(1))
    return pl.pallas_call(
        functools.partial(_ffn_kernel, mod_base=mod_base, final=final),
        grid=(b, s // tm),
        in_specs=[pl.BlockSpec((1, tm, d), lambda bi, i: (bi, i, 0)),
                  pl.BlockSpec((1, N_MOD, d), lambda bi, i: (bi, 0, 0)),
                  pl.BlockSpec((1, d), lambda bi, i: (0, 0)),
                  pl.BlockSpec((d, f), lambda bi, i: (0, 0), **resident),
                  pl.BlockSpec((d, f), lambda bi, i: (0, 0), **resident),
                  pl.BlockSpec((f, d), lambda bi, i: (0, 0), **resident),
                  pl.BlockSpec((1, d), lambda bi, i: (0, 0))],
        out_specs=pl.BlockSpec((1, tm, d), lambda bi, i: (bi, i, 0)),
        out_shape=jax.ShapeDtypeStruct((b, s, d), F32),
        name="ffn",
        compiler_params=_cparams(("arbitrary", "arbitrary")),
    )(x, mod, g.reshape(1, d), wg, wu, wd, final_g.reshape(1, d))


def _inproj_kernel(x_ref, mod_ref, g_ref, w_ref, o_ref):
    m = mod_ref[0]
    h = _modulate(x_ref[0], g_ref[...], m[3:4], m[4:5]).astype(BF16)
    o_ref[0] = _dot(h, w_ref[...])


def _inproj(x, mod, g, w, *, tm):
    b, s, d = x.shape
    n = w.shape[1]
    return pl.pallas_call(
        _inproj_kernel,
        grid=(b, s // tm),
        in_specs=[pl.BlockSpec((1, tm, d), lambda bi, i: (bi, i, 0)),
                  pl.BlockSpec((1, N_MOD, d), lambda bi, i: (bi, 0, 0)),
                  pl.BlockSpec((1, d), lambda bi, i: (0, 0)),
                  pl.BlockSpec((d, n), lambda bi, i: (0, 0), pipeline_mode=pl.Buffered(1))],
        out_specs=pl.BlockSpec((1, tm, n), lambda bi, i: (bi, i, 0)),
        out_shape=jax.ShapeDtypeStruct((b, s, n), F32),
        name="in_proj",
        compiler_params=_cparams(("arbitrary", "arbitrary")),
    )(x, mod, g.reshape(1, d), w)


def _conv_kernel(u_ref, p_ref, w_ref, b_ref, lg_ref, lb_ref, o_ref, hs_ref, *, ts, ch):
    i = pl.program_id(1)
    cur = u_ref[0]
    prev = p_ref[0]
    hprev = prev[:, :ch] * jax.nn.sigmoid(prev[:, ch:])
    hs_ref[0:CONV_HALO, :] = jnp.where(i > 0, hprev, 0.0)
    hs_ref[CONV_HALO:CONV_HALO + ts, :] = cur[:, :ch] * jax.nn.sigmoid(cur[:, ch:])
    base = CONV_HALO - (CONV_K - 1)
    acc = jnp.zeros((ts, ch), F32) + b_ref[...]
    for k in range(CONV_K):
        acc = acc + hs_ref[pl.ds(base + k, ts), :] * w_ref[k:k + 1, :]
    mu = jnp.mean(acc, axis=-1, keepdims=True)
    cen = acc - mu
    var = jnp.mean(cen * cen, axis=-1, keepdims=True)
    o_ref[0] = _silu(cen * lax.rsqrt(var + EPS) * lg_ref[...] + lb_ref[...])


def _conformer_conv(u, w, bias, ln_g, ln_b, *, ts):
    b, s, _ = u.shape
    ch = w.shape[1]
    hb = ts // CONV_HALO
    return pl.pallas_call(
        functools.partial(_conv_kernel, ts=ts, ch=ch),
        grid=(b, s // ts),
        in_specs=[pl.BlockSpec((1, ts, 2 * ch), lambda bi, i: (bi, i, 0)),
                  pl.BlockSpec((1, CONV_HALO, 2 * ch), lambda bi, i: (bi, jnp.maximum(i * hb - 1, 0), 0)),
                  pl.BlockSpec((CONV_K, ch), lambda bi, i: (0, 0)),
                  pl.BlockSpec((1, ch), lambda bi, i: (0, 0)),
                  pl.BlockSpec((1, ch), lambda bi, i: (0, 0)),
                  pl.BlockSpec((1, ch), lambda bi, i: (0, 0))],
        out_specs=pl.BlockSpec((1, ts, ch), lambda bi, i: (bi, i, 0)),
        out_shape=jax.ShapeDtypeStruct((b, s, ch), F32),
        scratch_shapes=[pltpu.VMEM((CONV_HALO + ts, ch), F32)],
        name="conformer_conv",
        compiler_params=_cparams(("arbitrary", "arbitrary")),
    )(u, u, w, bias.reshape(1, ch), ln_g.reshape(1, ch), ln_b.reshape(1, ch))


def _rope_tables(s):
    half = ROPE_DIM // 2
    inv = jnp.exp(-math.log(ROPE_THETA) * jnp.arange(0, ROPE_DIM, 2, dtype=F32) / ROPE_DIM)
    ang = jnp.arange(s, dtype=F32)[:, None] * inv[None, :]
    cos, sin = jnp.cos(ang), jnp.sin(ang)
    ones = jnp.ones((s, HEAD_DIM - ROPE_DIM), F32)
    zeros = jnp.zeros((s, HEAD_DIM - ROPE_DIM), F32)
    zh = jnp.zeros((s, half), F32)
    c = jnp.concatenate([cos, cos, ones], axis=-1)
    sa = jnp.concatenate([-sin, zh, zeros], axis=-1)
    sb = jnp.concatenate([zh, sin, zeros], axis=-1)
    two = lambda t: jnp.concatenate([t, t], axis=-1)
    return two(c), two(sa), two(sb)


def _moba_prep_kernel(q_ref, k_ref, v_ref, c_ref, sa_ref, sb_ref, qo_ref, ko_ref, vo_ref, km_ref):
    i = pl.program_id(2)
    half = ROPE_DIM // 2
    c, sa, sb = c_ref[...], sa_ref[...], sb_ref[...]

    def rope(x):
        return x * c + pltpu.roll(x, LANES - half, 1) * sa + pltpu.roll(x, half, 1) * sb

    qr, kr, v = rope(q_ref[0]), rope(k_ref[0]), v_ref[0]
    lane = _iota(qr.shape, 1)
    low = lane < HEAD_DIM
    onehot = jnp.where(lane == HEAD_DIM + i, 1.0, 0.0)
    ones_col = jnp.where(lane == HEAD_DIM, 1.0, 0.0)
    for hh in range(2):
        if hh == 0:
            qh, kh, vh = qr, kr, v
        else:
            qh, kh, vh = (pltpu.roll(t, HEAD_DIM, 1) for t in (qr, kr, v))
        kz = jnp.where(low, kh, 0.0)
        qo_ref[0, hh] = jnp.where(low, qh, 0.0)
        ko_ref[0, hh] = jnp.where(low, kh, onehot).astype(BF16)
        vo_ref[0, hh] = jnp.where(low, vh, ones_col).astype(BF16)
        km_ref[0, hh, 0] = jnp.mean(kz, axis=0, keepdims=True)


def _moba_prep(u, tables, *, q_lane_block):
    b, s, _ = u.shape
    nb = s // MOBA_BLOCK
    ts = MOBA_BLOCK
    c, sa, sb = tables
    qb = q_lane_block
    tab = pl.BlockSpec((ts, LANES), lambda bi, p, i: (i, 0))
    head = lambda dt: jax.ShapeDtypeStruct((b, 6, s, LANES), dt)
    return pl.pallas_call(
        _moba_prep_kernel,
        grid=(b, 3, nb),
        in_specs=[pl.BlockSpec((1, ts, LANES), lambda bi, p, i: (bi, i, qb + p)),
                  pl.BlockSpec((1, ts, LANES), lambda bi, p, i: (bi, i, qb + 3 + p)),
                  pl.BlockSpec((1, ts, LANES), lambda bi, p, i: (bi, i, qb + 6 + p)),
                  tab, tab, tab],
        out_specs=[pl.BlockSpec((1, 2, ts, LANES), lambda bi, p, i: (bi, p, i, 0)),
                   pl.BlockSpec((1, 2, ts, LANES), lambda bi, p, i: (bi, p, i, 0)),
                   pl.BlockSpec((1, 2, ts, LANES), lambda bi, p, i: (bi, p, i, 0)),
                   pl.BlockSpec((1, 2, 1, 1, LANES), lambda bi, p, i: (bi, p, i, 0, 0))],
        out_shape=[head(F32), head(BF16), head(BF16), jax.ShapeDtypeStruct((b, 6, nb, 1, LANES), F32)],
        name="moba_prep",
        compiler_params=_cparams(("arbitrary", "arbitrary", "arbitrary")),
    )(u, u, u, c, sa, sb)


def _moba_kernel(q_ref, ka_ref, va_ref, kmt_ref, o_ref):
    i = pl.program_id(2)
    blk = MOBA_BLOCK
    lane = _iota((blk, LANES), 1)
    lane_f = lane.astype(F32)
    gate_lane = (lane >= HEAD_DIM) & (lane < HEAD_DIM + i)
    row = _iota((blk, blk), 0)
    col = _iota((blk, blk), 1)
    neg_inf = -jnp.inf

    qas = []
    for hh in range(2):
        qp = q_ref[0, hh]
        gate = _dot_nt(qp, kmt_ref[0, hh], precision=HIGHEST)
        s = jnp.where(gate_lane, gate, neg_inf)
        sel = lane == HEAD_DIM + i
        for _ in range(MOBA_TOPK):
            m = jnp.max(s, axis=-1, keepdims=True)
            idx = jnp.min(jnp.where(s == m, lane_f, float(4 * LANES)), axis=-1, keepdims=True)
            hit = lane_f == idx
            sel = sel | (hit & (m > neg_inf))
            s = jnp.where(hit, neg_inf, s)
        negm = jnp.where((lane >= HEAD_DIM) & jnp.logical_not(sel), MASK_NEG, 0.0)
        qas.append((qp * (HEAD_DIM ** -0.5) + negm).astype(BF16))

    own = pl.multiple_of(i * blk, blk)
    ms, accs = [], []
    for hh in range(2):
        s0 = _dot_nt(qas[hh], ka_ref[0, hh, pl.ds(own, blk), :])
        s0 = jnp.where(col <= row, s0, MASK_NEG)
        m0 = jnp.max(s0, axis=-1, keepdims=True)
        p0 = jnp.exp(s0 - m0).astype(BF16)
        ms.append(m0)
        accs.append(_dot(p0, va_ref[0, hh, pl.ds(own, blk), :]))

    def body(j, carry):
        start = pl.multiple_of(j * blk, blk)
        out = []
        for hh in range(2):
            m, acc = carry[2 * hh], carry[2 * hh + 1]
            s = _dot_nt(qas[hh], ka_ref[0, hh, pl.ds(start, blk), :])
            mn = jnp.maximum(m, jnp.max(s, axis=-1, keepdims=True))
            p = jnp.exp(s - mn).astype(BF16)
            acc = jnp.exp(m - mn) * acc + _dot(p, va_ref[0, hh, pl.ds(start, blk), :])
            out += [mn, acc]
        return tuple(out)

    res = lax.fori_loop(0, i, body, (ms[0], accs[0], ms[1], accs[1]))
    o0 = res[1] / res[1][:, HEAD_DIM:HEAD_DIM + 1]
    o1 = res[3] / res[3][:, HEAD_DIM:HEAD_DIM + 1]
    o_ref[0, 0] = jnp.where(lane < HEAD_DIM, o0, pltpu.roll(o1, HEAD_DIM, 1))


def _moba_attention(qp, ka, va, kmt):
    b, _, s, _ = qp.shape
    nb = s // MOBA_BLOCK
    blk = MOBA_BLOCK
    return pl.pallas_call(
        _moba_kernel,
        grid=(b, 3, nb),
        in_specs=[pl.BlockSpec((1, 2, blk, LANES), lambda bi, p, i: (bi, p, i, 0)),
                  pl.BlockSpec((1, 2, s, LANES), lambda bi, p, i: (bi, p, 0, 0)),
                  pl.BlockSpec((1, 2, s, LANES), lambda bi, p, i: (bi, p, 0, 0)),
                  pl.BlockSpec((1, 2, LANES, LANES), lambda bi, p, i: (bi, p, 0, 0))],
        out_specs=pl.BlockSpec((1, 1, blk, LANES), lambda bi, p, i: (bi, p, i, 0)),
        out_shape=jax.ShapeDtypeStruct((b, 3, s, LANES), F32),
        name="moba_attention",
        compiler_params=_cparams(("arbitrary", "arbitrary", "arbitrary")),
    )(qp, ka, va, kmt)


def _gdn_prep_kernel(q_ref, k_ref, v_ref, hq_ref, hk_ref, hv_ref, ab_ref, wq_ref, wk_ref, wv_ref, al_ref, dt_ref,
                     qn_ref, kn_ref, kb_ref, vb_ref, kbg_ref, qg_ref, kd_ref, gc_ref, eg_ref, xs_ref, *, ts):
    i = pl.program_id(1)
    p = pl.program_id(2)
    base = GDN_HALO - (GDN_CONV_K - 1)

    def conv_silu(x_ref, h_ref, w_ref):
        xs_ref[0:GDN_HALO, :] = jnp.where(i > 0, h_ref[0], 0.0)
        xs_ref[GDN_HALO:GDN_HALO + ts, :] = x_ref[0]
        acc = jnp.zeros((ts, LANES), F32)
        for k in range(GDN_CONV_K):
            acc = acc + xs_ref[pl.ds(base + k, ts), :] * w_ref[k:k + 1, :]
        return _silu(acc)

    q, k, v = conv_silu(q_ref, hq_ref, wq_ref), conv_silu(k_ref, hk_ref, wk_ref), conv_silu(v_ref, hv_ref, wv_ref)

    r = _iota((LANES, LANES), 0)
    c = _iota((LANES, LANES), 1)
    head_sum = jnp.where(_div64(r) == _div64(c), 1.0, 0.0)

    def l2n(x):
        return x * lax.rsqrt(_dot_hi(x * x, head_sum) + EPS)

    qn = l2n(q) * (HEAD_DIM ** -0.5)
    kn = l2n(k)

    ab = ab_ref[0]
    lane = _iota(ab.shape, 1)
    nh = 6
    xa = ab + dt_ref[...]
    softplus = jnp.maximum(xa, 0.0) + jnp.log1p(jnp.exp(-jnp.abs(xa)))
    g_all = jnp.where(lane < nh, -jnp.exp(al_ref[...]) * softplus, 0.0)
    beta_all = jnp.where((lane >= nh) & (lane < 2 * nh), jax.nn.sigmoid(ab), 0.0)
    pick_g = jnp.where(r == 2 * p + _div64(c), 1.0, 0.0)
    pick_b = jnp.where(r == nh + 2 * p + _div64(c), 1.0, 0.0)
    gx = _dot_hi(g_all, pick_g)
    bx = _dot_hi(beta_all, pick_b)

    tr = _iota((ts, ts), 0)
    tc = _iota((ts, ts), 1)
    same = _div64(tr) == _div64(tc)
    gcx = _dot_hi(jnp.where(same & (tc <= tr), 1.0, 0.0), gx)
    glx = _dot_hi(jnp.where(same, 1.0, 0.0), gx)
    eg = jnp.exp(gcx)

    kb = kn * bx
    qn_ref[0, 0] = qn.astype(BF16)
    kn_ref[0, 0] = kn.astype(BF16)
    kb_ref[0, 0] = kb.astype(BF16)
    vb_ref[0, 0] = v * bx
    kbg_ref[0, 0] = kb * eg
    qg_ref[0, 0] = (qn * eg).astype(BF16)
    kd_ref[0, 0] = kn * jnp.exp(glx - gcx)
    gc_ref[0, 0] = gcx
    eg_ref[0, 0] = jnp.exp(glx)


def _gdn_prep(u, conv_w, a_log, dt_bias, *, q_lane_block, ab_lane_block, ts):
    b, s, _ = u.shape
    qb = q_lane_block
    hb = ts // GDN_HALO
    nh = a_log.shape[0]
    pad = lambda t: jnp.zeros((1, LANES), F32).at[0, :nh].set(t)
    cur = lambda off: pl.BlockSpec((1, ts, LANES), lambda bi, i, p: (bi, i, qb + off + p))
    halo = lambda off: pl.BlockSpec((1, GDN_HALO, LANES), lambda bi, i, p: (bi, jnp.maximum(i * hb - 1, 0), qb + off + p))
    wsp = lambda off: pl.BlockSpec((GDN_CONV_K, LANES), lambda bi, i, p: (0, off + p))
    vec = pl.BlockSpec((1, LANES), lambda bi, i, p: (0, 0))
    osp = pl.BlockSpec((1, 1, ts, LANES), lambda bi, i, p: (bi, p, i, 0))
    sh = lambda dt: jax.ShapeDtypeStruct((b, 3, s, LANES), dt)
    return pl.pallas_call(
        functools.partial(_gdn_prep_kernel, ts=ts),
        grid=(b, s // ts, 3),
        in_specs=[cur(0), cur(3), cur(6), halo(0), halo(3), halo(6),
                  pl.BlockSpec((1, ts, LANES), lambda bi, i, p: (bi, i, ab_lane_block)),
                  wsp(0), wsp(3), wsp(6), vec, vec],
        out_specs=[osp] * 9,
        out_shape=[sh(BF16), sh(BF16), sh(BF16), sh(F32), sh(F32), sh(BF16), sh(F32), sh(F32), sh(F32)],
        scratch_shapes=[pltpu.VMEM((GDN_HALO + ts, LANES), F32)],
        name="gdn_prep",
        compiler_params=_cparams(("arbitrary", "arbitrary", "arbitrary")),
    )(u, u, u, u, u, u, u, conv_w, conv_w, conv_w, pad(a_log), pad(dt_bias))


def _stack2(top, bot):
    return jnp.concatenate([top, bot], axis=0)


def _gdn_solve_kernel(qn_ref, kn_ref, kb_ref, vb_ref, kbg_ref, kd_ref, gc_ref, x_ref, qk_ref, kdt_ref, *, chunks):
    cs = GDN_CHUNK
    n2 = 2 * cs
    lane = _iota((cs, LANES), 1)
    low = lane < HEAD_DIM
    r = _iota((n2, n2), 0)
    c = _iota((n2, n2), 1)
    bd = _div64(r) == _div64(c)
    eye = jnp.where(r == c, 1.0, 0.0)
    diag8 = (r >> 3) == (c >> 3)
    merge_masks = [((r >> (sh + 1)) == (c >> (sh + 1))) & (((r >> sh) & 1) == 1) & (((c >> sh) & 1) == 0)
                   for sh in (3, 4, 5)]

    def per_head_rows(t, fill=0.0):
        return _stack2(jnp.where(low, t, fill), jnp.where(low, fill, t))

    for ci in range(chunks):
        rows = pl.ds(ci * cs, cs)
        kn = per_head_rows(kn_ref[0, 0, rows, :])
        kb = per_head_rows(kb_ref[0, 0, rows, :])
        qn = per_head_rows(qn_ref[0, 0, rows, :])
        kk = _dot_nt(kb, kn)
        qk = _dot_nt(qn, kn)
        gc = gc_ref[0, 0, rows, :]
        gsw = pltpu.roll(gc, HEAD_DIM, 1)
        gcol = _stack2(jnp.where(low, gc, gsw), jnp.where(low, gsw, gc))
        decay = jnp.exp(jnp.minimum(gcol - gcol.T, 0.0))
        lmat = jnp.where(c < r, kk * decay, 0.0)
        qk_ref[0, 0, pl.ds(ci * n2, n2), :] = jnp.where(c <= r, qk * decay, 0.0).astype(BF16)

        kbg = kbg_ref[0, 0, rows, :]
        vsw = pltpu.roll(vb_ref[0, 0, rows, :], HEAD_DIM, 1)
        x = jnp.where(bd, _stack2(kbg, kbg), _stack2(vsw, vsw))
        ld = jnp.where(diag8, lmat, 0.0)
        ld2 = _dot_hi(ld, ld)
        ld4 = _dot_hi(ld2, ld2)
        t = eye - ld
        t = t + _dot_hi(t, ld2)
        t = t + _dot_hi(t, ld4)
        for lower_left in merge_masks:
            tb = t.astype(BF16)
            inner = _dot(jnp.where(lower_left, lmat, 0.0).astype(BF16), tb)
            t = t - _dot(tb, inner.astype(BF16))
        x_ref[0, 0, pl.ds(ci * n2, n2), :] = _dot(t.astype(BF16), x.astype(BF16))

        kd = kd_ref[0, 0, rows, :]
        kdt_ref[0, 0, pl.ds(ci * n2, n2), :] = jnp.where(bd, _stack2(kd, kd), 0.0).T.astype(BF16)


def _gdn_solve(qn, kn, kb, vb, kbg, kd, gc, *, ts):
    b, _, s, _ = qn.shape
    chunks = ts // GDN_CHUNK
    isp = pl.BlockSpec((1, 1, ts, LANES), lambda bi, p, i: (bi, p, i, 0))
    osp = pl.BlockSpec((1, 1, 2 * ts, LANES), lambda bi, p, i: (bi, p, i, 0))
    sh = lambda dt: jax.ShapeDtypeStruct((b, 3, 2 * s, LANES), dt)
    return pl.pallas_call(
        functools.partial(_gdn_solve_kernel, chunks=chunks),
        grid=(b, 3, s // ts),
        in_specs=[isp] * 7,
        out_specs=[osp] * 3,
        out_shape=[sh(F32), sh(BF16), sh(BF16)],
        name="gdn_solve",
        compiler_params=_cparams(("arbitrary", "arbitrary", "arbitrary")),
    )(qn, kn, kb, vb, kbg, kd, gc)


def _gdn_scan_kernel(x_ref, qk_ref, kdt_ref, qg_ref, eg_ref, z0_ref, z1_ref, z2_ref, ng_ref, o_ref, st_ref,
                     *, chunks, batch):
    cs = GDN_CHUNK
    n2 = 2 * cs
    r = _iota((n2, n2), 0)
    c = _iota((n2, n2), 1)
    bd = _div64(r) == _div64(c)
    hr = _iota((LANES, LANES), 0)
    hc = _iota((LANES, LANES), 1)
    head_mean = jnp.where(_div64(hr) == _div64(hc), 1.0 / HEAD_DIM, 0.0)
    z_refs = (z0_ref, z1_ref, z2_ref)

    @pl.when(pl.program_id(0) == 0)
    def _():
        st_ref[...] = jnp.zeros_like(st_ref)

    def body(ci, carry):
        start = pl.multiple_of(ci * cs, cs)
        rows = pl.ds(start, cs)
        rows2 = pl.ds(pl.multiple_of(ci * n2, n2), n2)
        for bi in range(batch):
            for p in range(3):
                sidx = bi * 3 + p
                st = st_ref[sidx]
                stb = st.astype(BF16)
                x = x_ref[bi, p, rows2, :]
                w = jnp.where(bd, x, 0.0).astype(BF16)
                vn = jnp.where(bd, 0.0, x) - _dot(w, stb)
                vnb = vn.astype(BF16)
                qg = qg_ref[bi, p, rows, :]
                qbd = jnp.where(bd, _stack2(qg, qg), jnp.zeros((), BF16))
                o = _dot(qbd, stb) + _dot(qk_ref[bi, p, rows2, :], vnb)
                eg = pltpu.roll(eg_ref[bi, p, pl.ds(start, 8), :], HEAD_DIM, 1)[0:1, :]
                st_ref[sidx] = st * eg + _dot(kdt_ref[bi, p, rows2, :], vnb)
                op = pltpu.roll(o[:cs] + o[cs:], HEAD_DIM, 1)
                ms = _dot_hi(op * op, head_mean)
                z = z_refs[p][bi, rows, :]
                o_ref[bi, p, rows, :] = op * lax.rsqrt(ms + EPS) * ng_ref[...] * _silu(z)
        return carry

    lax.fori_loop(0, chunks, body, 0)


def _gdn_scan(x, qk, kdt, qg, eg, u, norm_g, *, z_lane_block, ts):
    b, _, s, _ = qg.shape
    chunks = ts // GDN_CHUNK
    zb = z_lane_block
    two = pl.BlockSpec((b, 3, 2 * ts, LANES), lambda i: (0, 0, i, 0))
    one = pl.BlockSpec((b, 3, ts, LANES), lambda i: (0, 0, i, 0))
    zsp = lambda p: pl.BlockSpec((b, ts, LANES), lambda i: (0, i, zb + p))
    ng = jnp.concatenate([norm_g, norm_g]).reshape(1, LANES)
    return pl.pallas_call(
        functools.partial(_gdn_scan_kernel, chunks=chunks, batch=b),
        grid=(s // ts,),
        in_specs=[two, two, two, one, one, zsp(0), zsp(1), zsp(2), pl.BlockSpec((1, LANES), lambda i: (0, 0))],
        out_specs=one,
        out_shape=jax.ShapeDtypeStruct((b, 3, s, LANES), F32),
        scratch_shapes=[pltpu.VMEM((b * 3, LANES, LANES), F32)],
        name="gdn_scan",
        compiler_params=_cparams(("arbitrary",)),
    )(x, qk, kdt, qg, eg, u, u, u, ng)


def _outproj_kernel(x_ref, mod_ref, yc_ref, ym_ref, yg_ref, w_ref, o_ref, *, conv_ch):
    acc = _dot(yc_ref[0].astype(BF16), w_ref[0:conv_ch, :])
    off = conv_ch
    for y_ref in (ym_ref, yg_ref):
        for p in range(3):
            acc = acc + _dot(y_ref[0, p].astype(BF16), w_ref[off:off + LANES, :])
            off += LANES
    o_ref[0] = x_ref[0] + mod_ref[0][5:6] * acc


def _outproj(x, mod, yc, ym, yg, w, *, tm):
    b, s, d = x.shape
    ch = yc.shape[-1]
    return pl.pallas_call(
        functools.partial(_outproj_kernel, conv_ch=ch),
        grid=(b, s // tm),
        in_specs=[pl.BlockSpec((1, tm, d), lambda bi, i: (bi, i, 0)),
                  pl.BlockSpec((1, N_MOD, d), lambda bi, i: (bi, 0, 0)),
                  pl.BlockSpec((1, tm, ch), lambda bi, i: (bi, i, 0)),
                  pl.BlockSpec((1, 3, tm, LANES), lambda bi, i: (bi, 0, i, 0)),
                  pl.BlockSpec((1, 3, tm, LANES), lambda bi, i: (bi, 0, i, 0)),
                  pl.BlockSpec(w.shape, lambda bi, i: (0, 0), pipeline_mode=pl.Buffered(1))],
        out_specs=pl.BlockSpec((1, tm, d), lambda bi, i: (bi, i, 0)),
        out_shape=jax.ShapeDtypeStruct((b, s, d), F32),
        name="out_proj",
        compiler_params=_cparams(("arbitrary", "arbitrary")),
    )(x, mod, yc, ym, yg, w)


def _token_mixing(x, mod, ln_g, w_in, conv_w, conv_b, conv_ln_g, conv_ln_b, gdn_conv_w, gdn_a_log, gdn_dt_bias,
                  gdn_norm_g, w_out, tables, *, tm, ts_conv, ts_gdn, ts_scan):
    d, d_in = w_in.shape
    conv_ch = conv_w.shape[1]
    n_pad = -(-d_in // LANES) * LANES
    w_in_p = jnp.zeros((d, n_pad), BF16).at[:, :d_in].set(w_in.astype(BF16))
    u = _inproj(x, mod, ln_g, w_in_p, tm=tm)

    moba_w = 3 * LANES
    q_blk_moba = 2 * conv_ch // LANES
    q_blk_gdn = q_blk_moba + 3 * moba_w // LANES
    z_blk = q_blk_gdn + 9
    ab_blk = z_blk + 3

    y_conv = _conformer_conv(u, conv_w, conv_b, conv_ln_g, conv_ln_b, ts=ts_conv)

    qp, ka, va, km = _moba_prep(u, tables, q_lane_block=q_blk_moba)
    b, _, nb, _, _ = km.shape
    kmt = jnp.zeros((b, 6, LANES, LANES), F32).at[:, :, HEAD_DIM:HEAD_DIM + nb, :].set(km[:, :, :, 0, :])
    y_moba = _moba_attention(qp, ka, va, kmt)

    qn, kn, kb, vb, kbg, qg, kd, gc, eg = _gdn_prep(u, gdn_conv_w, gdn_a_log, gdn_dt_bias, q_lane_block=q_blk_gdn,
                                                     ab_lane_block=ab_blk, ts=ts_gdn)
    xs, qk, kdt = _gdn_solve(qn, kn, kb, vb, kbg, kd, gc, ts=ts_gdn)
    y_gdn = _gdn_scan(xs, qk, kdt, qg, eg, u, gdn_norm_g, z_lane_block=z_blk, ts=ts_scan)

    return _outproj(x, mod, y_conv, y_moba, y_gdn, w_out.astype(BF16), tm=tm)


def kernel(x, c, w_ada, b_ada, ln_ffn1_g, ffn1_w_gate, ffn1_w_up, ffn1_w_down, ln_mix_g, w_in, conv_w, conv_b, conv_ln_g, conv_ln_b, gdn_conv_w, gdn_a_log, gdn_dt_bias, gdn_norm_g, w_out, ln_ffn2_g, ffn2_w_gate, ffn2_w_up, ffn2_w_down, final_g):
    depth = w_ada.shape[0]
    s = x.shape[1]
    tm = min(512, s)
    tiles = dict(tm=tm, ts_conv=min(512, s), ts_gdn=min(256, s), ts_scan=min(512, s))
    mods = _ada_mod(c, w_ada, b_ada)
    tables = _rope_tables(s)
    bf = lambda t: t.astype(BF16)
    for l in range(depth):
        mod = mods[l]
        x = _ffn(x, mod, ln_ffn1_g[l], bf(ffn1_w_gate[l]), bf(ffn1_w_up[l]), bf(ffn1_w_down[l]), final_g,
                 mod_base=0, final=False, tm=tm)
        x = _token_mixing(x, mod, ln_mix_g[l], w_in[l], conv_w[l], conv_b[l], conv_ln_g[l], conv_ln_b[l],
                          gdn_conv_w[l], gdn_a_log[l], gdn_dt_bias[l], gdn_norm_g[l], w_out[l], tables, **tiles)
        x = _ffn(x, mod, ln_ffn2_g[l], bf(ffn2_w_gate[l]), bf(ffn2_w_up[l]), bf(ffn2_w_down[l]), final_g,
                 mod_base=6, final=(l == depth - 1), tm=tm)
    return x
```

```python
import functools
import math

import jax
import jax.numpy as jnp
from jax import lax
from jax.experimental import pallas as pl
from jax.experimental.pallas import tpu as pltpu

F32 = jnp.float32
BF16 = jnp.bfloat16
HIGHEST = lax.Precision.HIGHEST

EPS = 1e-6
HEAD_DIM = 64
LANES = 128
N_MOD = 9
CONV_K = 31
CONV_HALO = 32
MOBA_BLOCK = 256
MOBA_TOPK = 3
MOBA_GROUP = 4
ROPE_DIM = HEAD_DIM // 4
ROPE_THETA = 500000.0
GDN_CONV_K = 4
GDN_HALO = 8
GDN_CHUNK = 64
MASK_NEG = -(2.0 ** 100)
VMEM_LIMIT = 56 * 1024 * 1024


def _cparams(sem):
    return pltpu.CompilerParams(dimension_semantics=sem, vmem_limit_bytes=VMEM_LIMIT)


def _iota(shape, dim):
    return lax.broadcasted_iota(jnp.int32, shape, dim)


def _div64(t):
    return t >> 6


def _dot(a, b):
    return jnp.dot(a, b, preferred_element_type=F32)


def _dot_hi(a, b):
    return jnp.dot(a, b, preferred_element_type=F32, precision=HIGHEST)


def _dot_nt(a, b, precision=None):
    return lax.dot_general(a, b, (((1,), (1,)), ((), ())), preferred_element_type=F32, precision=precision)


def _split3(x):
    x0 = x.astype(BF16)
    r1 = x - x0.astype(F32)
    x1 = r1.astype(BF16)
    x2 = (r1 - x1.astype(F32)).astype(BF16)
    return x0, x1, x2


def _dot_nt_split3(a, b):
    a0, a1, a2 = _split3(a)
    b0, b1, b2 = _split3(b)
    small = _dot_nt(a0, b2) + _dot_nt(a1, b1) + _dot_nt(a2, b0)
    mid = _dot_nt(a0, b1) + _dot_nt(a1, b0)
    return _dot_nt(a0, b0) + (mid + small)


def _dot_split2(a, b):
    a0 = a.astype(BF16)
    a1 = (a - a0.astype(F32)).astype(BF16)
    b0 = b.astype(BF16)
    b1 = (b - b0.astype(F32)).astype(BF16)
    return _dot(a0, b0) + (_dot(a0, b1) + _dot(a1, b0))


def _silu(x):
    return x * jax.nn.sigmoid(x)


def _modulate(x, g, shift, scale):
    ms = jnp.mean(x * x, axis=-1, keepdims=True)
    y = x * lax.rsqrt(ms + EPS) * g
    return y * (1.0 + scale) + shift


def _ada_kernel(c_ref, w_ref, b_ref, o_ref):
    ca = _silu(c_ref[...]).astype(BF16)
    o_ref[0] = _dot(ca, w_ref[0].astype(BF16)) + b_ref[0]


def _ada_mod(c, w_ada, b_ada):
    depth, d, nd = w_ada.shape
    b = c.shape[0]
    rows = 8
    c8 = jnp.zeros((rows, d), F32).at[:b].set(c)
    out = pl.pallas_call(
        _ada_kernel,
        grid=(depth, nd // d),
        in_specs=[pl.BlockSpec((rows, d), lambda l, j: (0, 0)),
                  pl.BlockSpec((1, d, d), lambda l, j: (l, 0, j)),
                  pl.BlockSpec((1, 1, d), lambda l, j: (l, 0, j))],
        out_specs=pl.BlockSpec((1, rows, d), lambda l, j: (l, 0, j)),
        out_shape=jax.ShapeDtypeStruct((depth, rows, nd), F32),
        name="ada_mod",
        compiler_params=_cparams(("arbitrary", "arbitrary")),
    )(c8, w_ada, b_ada.reshape(depth, 1, nd))
    return out[:, :b].reshape(depth, b, nd // d, d)


def _ffn_kernel(x_ref, mod_ref, g_ref, wg_ref, wu_ref, wd_ref, fg_ref, o_ref, *, mod_base, final):
    x = x_ref[0]
    m = mod_ref[0]
    shift, scale, gate = m[mod_base:mod_base + 1], m[mod_base + 1:mod_base + 2], m[mod_base + 2:mod_base + 3]
    h = _modulate(x, g_ref[...], shift, scale).astype(BF16)
    a = _dot(h, wg_ref[...])
    u = _dot(h, wu_ref[...])
    act = (_silu(a) * u).astype(BF16)
    y = x + (0.5 * gate) * _dot(act, wd_ref[...])
    if final:
        y = y * lax.rsqrt(jnp.mean(y * y, axis=-1, keepdims=True) + EPS) * fg_ref[...]
    o_ref[0] = y


def _ffn(x, mod, g, wg, wu, wd, final_g, *, mod_base, final, tm):
    b, s, d = x.shape
    f = wg.shape[1]
    resident = dict(pipeline_mode=pl.Buffered(1))
    return pl.pallas_call(
        functools.partial(_ffn_kernel, mod_base=mod_base, final=final),
        grid=(b, s // tm),
        in_specs=[pl.BlockSpec((1, tm, d), lambda bi, i: (bi, i, 0)),
                  pl.BlockSpec((1, N_MOD, d), lambda bi, i: (bi, 0, 0)),
                  pl.BlockSpec((1, d), lambda bi, i: (0, 0)),
                  pl.BlockSpec((d, f), lambda bi, i: (0, 0), **resident),
                  pl.BlockSpec((d, f), lambda bi, i: (0, 0), **resident),
                  pl.BlockSpec((f, d), lambda bi, i: (0, 0), **resident),
                  pl.BlockSpec((1, d), lambda bi, i: (0, 0))],
        out_specs=pl.BlockSpec((1, tm, d), lambda bi, i: (bi, i, 0)),
        out_shape=jax.ShapeDtypeStruct((b, s, d), F32),
        name="ffn",
        compiler_params=_cparams(("arbitrary", "arbitrary")),
    )(x, mod, g.reshape(1, d), wg, wu, wd, final_g.reshape(1, d))


def _inproj_kernel(x_ref, mod_ref, g_ref, w_ref, o_ref):
    m = mod_ref[0]
    h = _modulate(x_ref[0], g_ref[...], m[3:4], m[4:5]).astype(BF16)
    o_ref[0] = _dot(h, w_ref[...])


def _inproj(x, mod, g, w, *, tm):
    b, s, d = x.shape
    n = w.shape[1]
    return pl.pallas_call(
        _inproj_kernel,
        grid=(b, s // tm),
        in_specs=[pl.BlockSpec((1, tm, d), lambda bi, i: (bi, i, 0)),
                  pl.BlockSpec((1, N_MOD, d), lambda bi, i: (bi, 0, 0)),
                  pl.BlockSpec((1, d), lambda bi, i: (0, 0)),
                  pl.BlockSpec((d, n), lambda bi, i: (0, 0), pipeline_mode=pl.Buffered(1))],
        out_specs=pl.BlockSpec((1, tm, n), lambda bi, i: (bi, i, 0)),
        out_shape=jax.ShapeDtypeStruct((b, s, n), F32),
        name="in_proj",
        compiler_params=_cparams(("arbitrary", "arbitrary")),
    )(x, mod, g.reshape(1, d), w)


def _conv_kernel(u_ref, p_ref, w_ref, b_ref, lg_ref, lb_ref, o_ref, hs_ref, *, ts, ch):
    i = pl.program_id(1)
    cur = u_ref[0]
    prev = p_ref[0]
    hprev = prev[:, :ch] * jax.nn.sigmoid(prev[:, ch:])
    hs_ref[0:CONV_HALO, :] = jnp.where(i > 0, hprev, 0.0)
    hs_ref[CONV_HALO:CONV_HALO + ts, :] = cur[:, :ch] * jax.nn.sigmoid(cur[:, ch:])
    base = CONV_HALO - (CONV_K - 1)
    acc = jnp.zeros((ts, ch), F32) + b_ref[...]
    for k in range(CONV_K):
        acc = acc + hs_ref[pl.ds(base + k, ts), :] * w_ref[k:k + 1, :]
    mu = jnp.mean(acc, axis=-1, keepdims=True)
    cen = acc - mu
    var = jnp.mean(cen * cen, axis=-1, keepdims=True)
    o_ref[0] = _silu(cen * lax.rsqrt(var + EPS) * lg_ref[...] + lb_ref[...])


def _conformer_conv(u, w, bias, ln_g, ln_b, *, ts):
    b, s, _ = u.shape
    ch = w.shape[1]
    hb = ts // CONV_HALO
    return pl.pallas_call(
        functools.partial(_conv_kernel, ts=ts, ch=ch),
        grid=(b, s // ts),
        in_specs=[pl.BlockSpec((1, ts, 2 * ch), lambda bi, i: (bi, i, 0)),
                  pl.BlockSpec((1, CONV_HALO, 2 * ch), lambda bi, i: (bi, jnp.maximum(i * hb - 1, 0), 0)),
                  pl.BlockSpec((CONV_K, ch), lambda bi, i: (0, 0)),
                  pl.BlockSpec((1, ch), lambda bi, i: (0, 0)),
                  pl.BlockSpec((1, ch), lambda bi, i: (0, 0)),
                  pl.BlockSpec((1, ch), lambda bi, i: (0, 0))],
        out_specs=pl.BlockSpec((1, ts, ch), lambda bi, i: (bi, i, 0)),
        out_shape=jax.ShapeDtypeStruct((b, s, ch), F32),
        scratch_shapes=[pltpu.VMEM((CONV_HALO + ts, ch), F32)],
        name="conformer_conv",
        compiler_params=_cparams(("arbitrary", "arbitrary")),
    )(u, u, w, bias.reshape(1, ch), ln_g.reshape(1, ch), ln_b.reshape(1, ch))


def _rope_tables(s):
    half = ROPE_DIM // 2
    inv = jnp.exp(-math.log(ROPE_THETA) * jnp.arange(0, ROPE_DIM, 2, dtype=F32) / ROPE_DIM)
    ang = jnp.arange(s, dtype=F32)[:, None] * inv[None, :]
    cos, sin = jnp.cos(ang), jnp.sin(ang)
    ones = jnp.ones((s, HEAD_DIM - ROPE_DIM), F32)
    zeros = jnp.zeros((s, HEAD_DIM - ROPE_DIM), F32)
    zh = jnp.zeros((s, half), F32)
    c = jnp.concatenate([cos, cos, ones], axis=-1)
    sa = jnp.concatenate([-sin, zh, zeros], axis=-1)
    sb = jnp.concatenate([zh, sin, zeros], axis=-1)
    two = lambda t: jnp.concatenate([t, t], axis=-1)
    return two(c), two(sa), two(sb)


def _moba_prep_kernel(q_ref, k_ref, v_ref, c_ref, sa_ref, sb_ref, qo_ref, ko_ref, vo_ref, km_ref):
    i = pl.program_id(2)
    half = ROPE_DIM // 2
    c, sa, sb = c_ref[...], sa_ref[...], sb_ref[...]

    def rope(x):
        return x * c + pltpu.roll(x, LANES - half, 1) * sa + pltpu.roll(x, half, 1) * sb

    qr, kr, v = rope(q_ref[0]), rope(k_ref[0]), v_ref[0]
    lane = _iota(qr.shape, 1)
    low = lane < HEAD_DIM
    onehot = jnp.where(lane == HEAD_DIM + i, 1.0, 0.0)
    ones_col = jnp.where(lane == HEAD_DIM, 1.0, 0.0)
    for hh in range(2):
        if hh == 0:
            qh, kh, vh = qr, kr, v
        else:
            qh, kh, vh = (pltpu.roll(t, HEAD_DIM, 1) for t in (qr, kr, v))
        kz = jnp.where(low, kh, 0.0)
        qo_ref[0, hh] = jnp.where(low, qh, 0.0)
        ko_ref[0, hh] = jnp.where(low, kh, onehot).astype(BF16)
        vo_ref[0, hh] = jnp.where(low, vh, ones_col).astype(BF16)
        km_ref[0, hh, 0] = jnp.mean(kz, axis=0, keepdims=True)


def _moba_prep(u, tables, *, q_lane_block):
    b, s, _ = u.shape
    nb = s // MOBA_BLOCK
    ts = MOBA_BLOCK
    c, sa, sb = tables
    qb = q_lane_block
    tab = pl.BlockSpec((ts, LANES), lambda bi, p, i: (i, 0))
    head = lambda dt: jax.ShapeDtypeStruct((b, 6, s, LANES), dt)
    return pl.pallas_call(
        _moba_prep_kernel,
        grid=(b, 3, nb),
        in_specs=[pl.BlockSpec((1, ts, LANES), lambda bi, p, i: (bi, i, qb + p)),
                  pl.BlockSpec((1, ts, LANES), lambda bi, p, i: (bi, i, qb + 3 + p)),
                  pl.BlockSpec((1, ts, LANES), lambda bi, p, i: (bi, i, qb + 6 + p)),
                  tab, tab, tab],
        out_specs=[pl.BlockSpec((1, 2, ts, LANES), lambda bi, p, i: (bi, p, i, 0)),
                   pl.BlockSpec((1, 2, ts, LANES), lambda bi, p, i: (bi, p, i, 0)),
                   pl.BlockSpec((1, 2, ts, LANES), lambda bi, p, i: (bi, p, i, 0)),
                   pl.BlockSpec((1, 2, 1, 1, LANES), lambda bi, p, i: (bi, p, i, 0, 0))],
        out_shape=[head(F32), head(BF16), head(BF16), jax.ShapeDtypeStruct((b, 6, nb, 1, LANES), F32)],
        name="moba_prep",
        compiler_params=_cparams(("arbitrary", "arbitrary", "arbitrary")),
    )(u, u, u, c, sa, sb)


def _moba_kernel(q_ref, ka_ref, va_ref, kmt_ref, o_ref):
    i = pl.program_id(2)
    blk = MOBA_BLOCK
    gk = MOBA_GROUP * blk
    lane = _iota((blk, LANES), 1)
    lane_f = lane.astype(F32)
    gate_lane = (lane >= HEAD_DIM) & (lane < HEAD_DIM + i)
    own_lane = jnp.where(lane == HEAD_DIM + i, 1.0, 0.0)
    mask_lanes = jnp.where(lane >= HEAD_DIM, MASK_NEG, 0.0)
    neg_inf = -jnp.inf

    qas = []
    for hh in range(2):
        qp = q_ref[0, hh]
        gate = _dot_nt_split3(qp, kmt_ref[0, hh])
        s = jnp.where(gate_lane, gate, neg_inf)
        sel = own_lane
        for _ in range(MOBA_TOPK):
            m = jnp.max(s, axis=-1, keepdims=True)
            idx = jnp.min(jnp.where(s == m, lane_f, float(4 * LANES)), axis=-1, keepdims=True)
            hit = lane_f == idx
            sel = jnp.where(hit, 1.0, sel)
            s = jnp.where(hit, neg_inf, s)
        negm = jnp.where(sel > 0.5, 0.0, mask_lanes)
        qas.append((qp * (HEAD_DIM ** -0.5 * math.log2(math.e)) + negm).astype(BF16))

    def step(carry, start, causal):
        hs = range(2)
        s = [_dot_nt(qas[hh], ka_ref[0, hh, pl.ds(start, gk), :]) for hh in hs]
        if causal:
            keep = start + _iota((blk, gk), 1) <= i * blk + _iota((blk, gk), 0)
            s = [jnp.where(keep, t, MASK_NEG) for t in s]
        mn = [jnp.maximum(carry[2 * hh], jnp.max(s[hh], axis=-1, keepdims=True)) for hh in hs]
        p = [jnp.exp2(s[hh] - mn[hh]).astype(BF16) for hh in hs]
        pv = [_dot(p[hh], va_ref[0, hh, pl.ds(start, gk), :]) for hh in hs]
        out = []
        for hh in hs:
            out += [mn[hh], jnp.exp2(carry[2 * hh] - mn[hh]) * carry[2 * hh + 1] + pv[hh]]
        return tuple(out)

    m_init = jnp.full((blk, 1), neg_inf, F32)
    acc_init = jnp.zeros((blk, LANES), F32)
    n_past = i >> (MOBA_GROUP.bit_length() - 1)
    res = lax.fori_loop(0, n_past, lambda g, c: step(c, pl.multiple_of(g * gk, gk), False),
                        (m_init, acc_init, m_init, acc_init))
    res = step(res, pl.multiple_of(n_past * gk, gk), True)
    o0 = res[1] / res[1][:, HEAD_DIM:HEAD_DIM + 1]
    o1 = res[3] / res[3][:, HEAD_DIM:HEAD_DIM + 1]
    o_ref[0, 0] = jnp.where(lane < HEAD_DIM, o0, pltpu.roll(o1, HEAD_DIM, 1))


def _moba_attention(qp, ka, va, kmt):
    b, _, s, _ = qp.shape
    nb = s // MOBA_BLOCK
    blk = MOBA_BLOCK
    return pl.pallas_call(
        _moba_kernel,
        grid=(b, 3, nb),
        in_specs=[pl.BlockSpec((1, 2, blk, LANES), lambda bi, p, i: (bi, p, i, 0)),
                  pl.BlockSpec((1, 2, s, LANES), lambda bi, p, i: (bi, p, 0, 0)),
                  pl.BlockSpec((1, 2, s, LANES), lambda bi, p, i: (bi, p, 0, 0)),
                  pl.BlockSpec((1, 2, LANES, LANES), lambda bi, p, i: (bi, p, 0, 0))],
        out_specs=pl.BlockSpec((1, 1, blk, LANES), lambda bi, p, i: (bi, p, i, 0)),
        out_shape=jax.ShapeDtypeStruct((b, 3, s, LANES), F32),
        name="moba_attention",
        compiler_params=_cparams(("arbitrary", "arbitrary", "arbitrary")),
    )(qp, ka, va, kmt)


def _gdn_prep_kernel(q_ref, k_ref, v_ref, hq_ref, hk_ref, hv_ref, ab_ref, wq_ref, wk_ref, wv_ref, al_ref, dt_ref,
                     qn_ref, kn_ref, kb_ref, vb_ref, kbg_ref, qg_ref, kd_ref, gc_ref, eg_ref, xs_ref, *, ts):
    i = pl.program_id(1)
    p = pl.program_id(2)
    base = GDN_HALO - (GDN_CONV_K - 1)

    def conv_silu(x_ref, h_ref, w_ref):
        xs_ref[0:GDN_HALO, :] = jnp.where(i > 0, h_ref[0], 0.0)
        xs_ref[GDN_HALO:GDN_HALO + ts, :] = x_ref[0]
        acc = jnp.zeros((ts, LANES), F32)
        for k in range(GDN_CONV_K):
            acc = acc + xs_ref[pl.ds(base + k, ts), :] * w_ref[k:k + 1, :]
        return _silu(acc)

    q, k, v = conv_silu(q_ref, hq_ref, wq_ref), conv_silu(k_ref, hk_ref, wk_ref), conv_silu(v_ref, hv_ref, wv_ref)

    r = _iota((LANES, LANES), 0)
    c = _iota((LANES, LANES), 1)
    head_sum = jnp.where(_div64(r) == _div64(c), 1.0, 0.0)

    def l2n(x):
        return x * lax.rsqrt(_dot_hi(x * x, head_sum) + EPS)

    qn = l2n(q) * (HEAD_DIM ** -0.5)
    kn = l2n(k)

    ab = ab_ref[0]
    lane = _iota(ab.shape, 1)
    nh = 6
    xa = ab + dt_ref[...]
    softplus = jnp.maximum(xa, 0.0) + jnp.log1p(jnp.exp(-jnp.abs(xa)))
    g_all = jnp.where(lane < nh, -jnp.exp(al_ref[...]) * softplus, 0.0)
    beta_all = jnp.where((lane >= nh) & (lane < 2 * nh), jax.nn.sigmoid(ab), 0.0)
    pick_g = jnp.where(r == 2 * p + _div64(c), 1.0, 0.0)
    pick_b = jnp.where(r == nh + 2 * p + _div64(c), 1.0, 0.0)
    gx = _dot_hi(g_all, pick_g)
    bx = _dot_hi(beta_all, pick_b)

    tr = _iota((ts, ts), 0)
    tc = _iota((ts, ts), 1)
    same = _div64(tr) == _div64(tc)
    gcx = _dot_hi(jnp.where(same & (tc <= tr), 1.0, 0.0), gx)
    glx = _dot_hi(jnp.where(same, 1.0, 0.0), gx)
    eg = jnp.exp(gcx)

    kb = kn * bx
    qn_ref[0, 0] = qn.astype(BF16)
    kn_ref[0, 0] = kn.astype(BF16)
    kb_ref[0, 0] = kb.astype(BF16)
    vb_ref[0, 0] = v * bx
    kbg_ref[0, 0] = kb * eg
    qg_ref[0, 0] = (qn * eg).astype(BF16)
    kd_ref[0, 0] = kn * jnp.exp(glx - gcx)
    gc_ref[0, 0] = gcx
    eg_ref[0, 0] = jnp.exp(glx)


def _gdn_prep(u, conv_w, a_log, dt_bias, *, q_lane_block, ab_lane_block, ts):
    b, s, _ = u.shape
    qb = q_lane_block
    hb = ts // GDN_HALO
    nh = a_log.shape[0]
    pad = lambda t: jnp.zeros((1, LANES), F32).at[0, :nh].set(t)
    cur = lambda off: pl.BlockSpec((1, ts, LANES), lambda bi, i, p: (bi, i, qb + off + p))
    halo = lambda off: pl.BlockSpec((1, GDN_HALO, LANES), lambda bi, i, p: (bi, jnp.maximum(i * hb - 1, 0), qb + off + p))
    wsp = lambda off: pl.BlockSpec((GDN_CONV_K, LANES), lambda bi, i, p: (0, off + p))
    vec = pl.BlockSpec((1, LANES), lambda bi, i, p: (0, 0))
    osp = pl.BlockSpec((1, 1, ts, LANES), lambda bi, i, p: (bi, p, i, 0))
    sh = lambda dt: jax.ShapeDtypeStruct((b, 3, s, LANES), dt)
    return pl.pallas_call(
        functools.partial(_gdn_prep_kernel, ts=ts),
        grid=(b, s // ts, 3),
        in_specs=[cur(0), cur(3), cur(6), halo(0), halo(3), halo(6),
                  pl.BlockSpec((1, ts, LANES), lambda bi, i, p: (bi, i, ab_lane_block)),
                  wsp(0), wsp(3), wsp(6), vec, vec],
        out_specs=[osp] * 9,
        out_shape=[sh(BF16), sh(BF16), sh(BF16), sh(F32), sh(F32), sh(BF16), sh(F32), sh(F32), sh(F32)],
        scratch_shapes=[pltpu.VMEM((GDN_HALO + ts, LANES), F32)],
        name="gdn_prep",
        compiler_params=_cparams(("arbitrary", "arbitrary", "arbitrary")),
    )(u, u, u, u, u, u, u, conv_w, conv_w, conv_w, pad(a_log), pad(dt_bias))


def _stack2(top, bot):
    return jnp.concatenate([top, bot], axis=0)


def _gdn_solve_kernel(qn_ref, kn_ref, kb_ref, vb_ref, kbg_ref, kd_ref, gc_ref, x_ref, qk_ref, kdt_ref, *, chunks):
    cs = GDN_CHUNK
    n2 = 2 * cs
    lane = _iota((cs, LANES), 1)
    low = lane < HEAD_DIM
    r = _iota((n2, n2), 0)
    c = _iota((n2, n2), 1)
    bd = _div64(r) == _div64(c)
    eye = jnp.where(r == c, 1.0, 0.0)
    diag8 = (r >> 3) == (c >> 3)
    merge_masks = [((r >> (sh + 1)) == (c >> (sh + 1))) & (((r >> sh) & 1) == 1) & (((c >> sh) & 1) == 0)
                   for sh in (3, 4, 5)]

    def per_head_rows(t, fill=0.0):
        return _stack2(jnp.where(low, t, fill), jnp.where(low, fill, t))

    cis = range(chunks)
    rows = [pl.ds(ci * cs, cs) for ci in cis]
    rows2 = [pl.ds(ci * n2, n2) for ci in cis]
    kn = [per_head_rows(kn_ref[0, 0, rw, :]) for rw in rows]
    kb = [per_head_rows(kb_ref[0, 0, rw, :]) for rw in rows]
    qn = [per_head_rows(qn_ref[0, 0, rw, :]) for rw in rows]
    kk = [_dot_nt(kb[ci], kn[ci]) for ci in cis]
    qk = [_dot_nt(qn[ci], kn[ci]) for ci in cis]
    decay = []
    for rw in rows:
        gc = gc_ref[0, 0, rw, :]
        gsw = pltpu.roll(gc, HEAD_DIM, 1)
        gcol = _stack2(jnp.where(low, gc, gsw), jnp.where(low, gsw, gc))
        decay.append(jnp.exp(jnp.minimum(gcol - gcol.T, 0.0)))
    lmat = [jnp.where(c < r, kk[ci] * decay[ci], 0.0) for ci in cis]
    for ci in cis:
        qk_ref[0, 0, rows2[ci], :] = jnp.where(c <= r, qk[ci] * decay[ci], 0.0).astype(BF16)

    ld = [jnp.where(diag8, lm, 0.0) for lm in lmat]
    ld2 = [_dot_split2(m, m) for m in ld]
    ld4 = [_dot_split2(m, m) for m in ld2]
    t = [eye - m for m in ld]
    t = [t[ci] + _dot_split2(t[ci], ld2[ci]) for ci in cis]
    t = [t[ci] + _dot_split2(t[ci], ld4[ci]) for ci in cis]
    for lower_left in merge_masks:
        tb = [m.astype(BF16) for m in t]
        inner = [_dot(jnp.where(lower_left, lmat[ci], 0.0).astype(BF16), tb[ci]) for ci in cis]
        t = [t[ci] - _dot(tb[ci], inner[ci].astype(BF16)) for ci in cis]
    for ci in cis:
        kbg = kbg_ref[0, 0, rows[ci], :]
        vsw = pltpu.roll(vb_ref[0, 0, rows[ci], :], HEAD_DIM, 1)
        x = jnp.where(bd, _stack2(kbg, kbg), _stack2(vsw, vsw))
        x_ref[0, 0, rows2[ci], :] = _dot(t[ci].astype(BF16), x.astype(BF16))
    for ci in cis:
        kd = kd_ref[0, 0, rows[ci], :]
        kdt_ref[0, 0, rows2[ci], :] = jnp.where(bd, _stack2(kd, kd), 0.0).T.astype(BF16)


def _gdn_solve(qn, kn, kb, vb, kbg, kd, gc, *, ts):
    b, _, s, _ = qn.shape
    chunks = ts // GDN_CHUNK
    isp = pl.BlockSpec((1, 1, ts, LANES), lambda bi, p, i: (bi, p, i, 0))
    osp = pl.BlockSpec((1, 1, 2 * ts, LANES), lambda bi, p, i: (bi, p, i, 0))
    sh = lambda dt: jax.ShapeDtypeStruct((b, 3, 2 * s, LANES), dt)
    return pl.pallas_call(
        functools.partial(_gdn_solve_kernel, chunks=chunks),
        grid=(b, 3, s // ts),
        in_specs=[isp] * 7,
        out_specs=[osp] * 3,
        out_shape=[sh(F32), sh(BF16), sh(BF16)],
        name="gdn_solve",
        compiler_params=_cparams(("arbitrary", "arbitrary", "arbitrary")),
    )(qn, kn, kb, vb, kbg, kd, gc)


def _gdn_scan_kernel(x_ref, qk_ref, kdt_ref, qg_ref, eg_ref, z0_ref, z1_ref, z2_ref, ng_ref, o_ref, st_ref,
                     *, chunks, batch):
    cs = GDN_CHUNK
    n2 = 2 * cs
    r = _iota((n2, n2), 0)
    c = _iota((n2, n2), 1)
    bd = _div64(r) == _div64(c)
    hr = _iota((LANES, LANES), 0)
    hc = _iota((LANES, LANES), 1)
    head_mean = jnp.where(_div64(hr) == _div64(hc), 1.0 / HEAD_DIM, 0.0)
    z_refs = (z0_ref, z1_ref, z2_ref)

    @pl.when(pl.program_id(0) == 0)
    def _():
        st_ref[...] = jnp.zeros_like(st_ref)

    def body(ci, carry):
        start = pl.multiple_of(ci * cs, cs)
        rows = pl.ds(start, cs)
        rows2 = pl.ds(pl.multiple_of(ci * n2, n2), n2)
        chains = [(bi, p) for bi in range(batch) for p in range(3)]
        ks = range(len(chains))
        st = [st_ref[bi * 3 + p] for bi, p in chains]
        stb = [t.astype(BF16) for t in st]
        x = [x_ref[bi, p, rows2, :] for bi, p in chains]
        ws = [_dot(jnp.where(bd, x[k], 0.0).astype(BF16), stb[k]) for k in ks]
        vnb = [(jnp.where(bd, 0.0, x[k]) - ws[k]).astype(BF16) for k in ks]
        qbd = [jnp.where(bd, _stack2(qg, qg), jnp.zeros((), BF16)) for qg in (qg_ref[bi, p, rows, :] for bi, p in chains)]
        o = [_dot(qbd[k], stb[k]) + _dot(qk_ref[bi, p, rows2, :], vnb[k]) for k, (bi, p) in enumerate(chains)]
        upd = [_dot(kdt_ref[bi, p, rows2, :], vnb[k]) for k, (bi, p) in enumerate(chains)]
        for k, (bi, p) in enumerate(chains):
            eg = pltpu.roll(eg_ref[bi, p, pl.ds(start, 8), :], HEAD_DIM, 1)[0:1, :]
            st_ref[bi * 3 + p] = st[k] * eg + upd[k]
        op = [pltpu.roll(t[:cs] + t[cs:], HEAD_DIM, 1) for t in o]
        ms = [_dot_hi(t * t, head_mean) for t in op]
        for k, (bi, p) in enumerate(chains):
            z = z_refs[p][bi, rows, :]
            o_ref[bi, p, rows, :] = op[k] * lax.rsqrt(ms[k] + EPS) * ng_ref[...] * _silu(z)
        return carry

    lax.fori_loop(0, chunks, body, 0)


def _gdn_scan(x, qk, kdt, qg, eg, u, norm_g, *, z_lane_block, ts):
    b, _, s, _ = qg.shape
    chunks = ts // GDN_CHUNK
    zb = z_lane_block
    two = pl.BlockSpec((b, 3, 2 * ts, LANES), lambda i: (0, 0, i, 0))
    one = pl.BlockSpec((b, 3, ts, LANES), lambda i: (0, 0, i, 0))
    zsp = lambda p: pl.BlockSpec((b, ts, LANES), lambda i: (0, i, zb + p))
    ng = jnp.concatenate([norm_g, norm_g]).reshape(1, LANES)
    return pl.pallas_call(
        functools.partial(_gdn_scan_kernel, chunks=chunks, batch=b),
        grid=(s // ts,),
        in_specs=[two, two, two, one, one, zsp(0), zsp(1), zsp(2), pl.BlockSpec((1, LANES), lambda i: (0, 0))],
        out_specs=one,
        out_shape=jax.ShapeDtypeStruct((b, 3, s, LANES), F32),
        scratch_shapes=[pltpu.VMEM((b * 3, LANES, LANES), F32)],
        name="gdn_scan",
        compiler_params=_cparams(("arbitrary",)),
    )(x, qk, kdt, qg, eg, u, u, u, ng)


def _outproj_kernel(x_ref, mod_ref, yc_ref, ym_ref, yg_ref, w_ref, o_ref, *, conv_ch):
    acc = _dot(yc_ref[0].astype(BF16), w_ref[0:conv_ch, :])
    off = conv_ch
    for y_ref in (ym_ref, yg_ref):
        for p in range(3):
            acc = acc + _dot(y_ref[0, p].astype(BF16), w_ref[off:off + LANES, :])
            off += LANES
    o_ref[0] = x_ref[0] + mod_ref[0][5:6] * acc


def _outproj(x, mod, yc, ym, yg, w, *, tm):
    b, s, d = x.shape
    ch = yc.shape[-1]
    return pl.pallas_call(
        functools.partial(_outproj_kernel, conv_ch=ch),
        grid=(b, s // tm),
        in_specs=[pl.BlockSpec((1, tm, d), lambda bi, i: (bi, i, 0)),
                  pl.BlockSpec((1, N_MOD, d), lambda bi, i: (bi, 0, 0)),
                  pl.BlockSpec((1, tm, ch), lambda bi, i: (bi, i, 0)),
                  pl.BlockSpec((1, 3, tm, LANES), lambda bi, i: (bi, 0, i, 0)),
                  pl.BlockSpec((1, 3, tm, LANES), lambda bi, i: (bi, 0, i, 0)),
                  pl.BlockSpec(w.shape, lambda bi, i: (0, 0), pipeline_mode=pl.Buffered(1))],
        out_specs=pl.BlockSpec((1, tm, d), lambda bi, i: (bi, i, 0)),
        out_shape=jax.ShapeDtypeStruct((b, s, d), F32),
        name="out_proj",
        compiler_params=_cparams(("arbitrary", "arbitrary")),
    )(x, mod, yc, ym, yg, w)


def _token_mixing(x, mod, ln_g, w_in, conv_w, conv_b, conv_ln_g, conv_ln_b, gdn_conv_w, gdn_a_log, gdn_dt_bias,
                  gdn_norm_g, w_out, tables, *, tm, ts_conv, ts_gdn, ts_solve, ts_scan):
    d, d_in = w_in.shape
    conv_ch = conv_w.shape[1]
    n_pad = -(-d_in // LANES) * LANES
    w_in_p = jnp.zeros((d, n_pad), BF16).at[:, :d_in].set(w_in.astype(BF16))
    u = _inproj(x, mod, ln_g, w_in_p, tm=tm)

    moba_w = 3 * LANES
    q_blk_moba = 2 * conv_ch // LANES
    q_blk_gdn = q_blk_moba + 3 * moba_w // LANES
    z_blk = q_blk_gdn + 9
    ab_blk = z_blk + 3

    y_conv = _conformer_conv(u, conv_w, conv_b, conv_ln_g, conv_ln_b, ts=ts_conv)

    qp, ka, va, km = _moba_prep(u, tables, q_lane_block=q_blk_moba)
    b, _, nb, _, _ = km.shape
    kmt = jnp.zeros((b, 6, LANES, LANES), F32).at[:, :, HEAD_DIM:HEAD_DIM + nb, :].set(km[:, :, :, 0, :])
    y_moba = _moba_attention(qp, ka, va, kmt)

    qn, kn, kb, vb, kbg, qg, kd, gc, eg = _gdn_prep(u, gdn_conv_w, gdn_a_log, gdn_dt_bias, q_lane_block=q_blk_gdn,
                                                     ab_lane_block=ab_blk, ts=ts_gdn)
    xs, qk, kdt = _gdn_solve(qn, kn, kb, vb, kbg, kd, gc, ts=ts_solve)
    y_gdn = _gdn_scan(xs, qk, kdt, qg, eg, u, gdn_norm_g, z_lane_block=z_blk, ts=ts_scan)

    return _outproj(x, mod, y_conv, y_moba, y_gdn, w_out.astype(BF16), tm=tm)


def kernel(x, c, w_ada, b_ada, ln_ffn1_g, ffn1_w_gate, ffn1_w_up, ffn1_w_down, ln_mix_g, w_in, conv_w, conv_b, conv_ln_g, conv_ln_b, gdn_conv_w, gdn_a_log, gdn_dt_bias, gdn_norm_g, w_out, ln_ffn2_g, ffn2_w_gate, ffn2_w_up, ffn2_w_down, final_g):
    depth = w_ada.shape[0]
    s = x.shape[1]
    tm = min(512, s)
    tiles = dict(tm=tm, ts_conv=min(512, s), ts_gdn=min(256, s), ts_solve=min(512, s), ts_scan=min(512, s))
    mods = _ada_mod(c, w_ada, b_ada)
    tables = _rope_tables(s)
    bf = lambda t: t.astype(BF16)
    for l in range(depth):
        mod = mods[l]
        x = _ffn(x, mod, ln_ffn1_g[l], bf(ffn1_w_gate[l]), bf(ffn1_w_up[l]), bf(ffn1_w_down[l]), final_g,
                 mod_base=0, final=False, tm=tm)
        x = _token_mixing(x, mod, ln_mix_g[l], w_in[l], conv_w[l], conv_b[l], conv_ln_g[l], conv_ln_b[l],
                          gdn_conv_w[l], gdn_a_log[l], gdn_dt_bias[l], gdn_norm_g[l], w_out[l], tables, **tiles)
        x = _ffn(x, mod, ln_ffn2_g[l], bf(ffn2_w_gate[l]), bf(ffn2_w_up[l]), bf(ffn2_w_down[l]), final_g,
                 mod_base=6, final=(l == depth - 1), tm=tm)
    return x
```

```python
import functools
import math

import numpy as np
import jax
import jax.numpy as jnp
from jax import lax
from jax.experimental import pallas as pl
from jax.experimental.pallas import tpu as pltpu

F32 = jnp.float32
BF16 = jnp.bfloat16

EPS = 1e-6
HEAD_DIM = 64
LANES = 128
N_MOD = 9
FFN_CHUNK = 256
CONV_K = 31
CONV_HALO = 32
MOBA_BLOCK = 256
MOBA_TOPK = 3
MOBA_GROUP = 4
ROPE_DIM = HEAD_DIM // 4
ROPE_THETA = 500000.0
GDN_CONV_K = 4
GDN_HALO = 8
GDN_CHUNK = 64
MASK_NEG = -(2.0 ** 100)
VMEM_LIMIT = 56 * 1024 * 1024


def _cparams(sem):
    return pltpu.CompilerParams(dimension_semantics=sem, vmem_limit_bytes=VMEM_LIMIT)


def _iota(shape, dim):
    return lax.broadcasted_iota(jnp.int32, shape, dim)


def _div64(t):
    return t >> 6


def _dot(a, b):
    return jnp.dot(a, b, preferred_element_type=F32)


def _dot_nt(a, b):
    return lax.dot_general(a, b, (((1,), (1,)), ((), ())), preferred_element_type=F32)


def _split3(x):
    x0 = x.astype(BF16)
    r1 = x - x0.astype(F32)
    x1 = r1.astype(BF16)
    x2 = (r1 - x1.astype(F32)).astype(BF16)
    return x0, x1, x2


def _dot_nt_split3(a, b):
    a0, a1, a2 = _split3(a)
    b0, b1, b2 = _split3(b)
    small = _dot_nt(a0, b2) + _dot_nt(a1, b1) + _dot_nt(a2, b0)
    mid = _dot_nt(a0, b1) + _dot_nt(a1, b0)
    return _dot_nt(a0, b0) + (mid + small)


def _dot_data_01(x, m01):
    x0, x1, x2 = _split3(x)
    return _dot(x0, m01) + (_dot(x1, m01) + _dot(x2, m01))


def _dot_01_data(m01, x):
    x0, x1, x2 = _split3(x)
    return _dot(m01, x0) + (_dot(m01, x1) + _dot(m01, x2))


def _dot_split2(a, b):
    a0 = a.astype(BF16)
    a1 = (a - a0.astype(F32)).astype(BF16)
    b0 = b.astype(BF16)
    b1 = (b - b0.astype(F32)).astype(BF16)
    return _dot(a0, b0) + (_dot(a0, b1) + _dot(a1, b0))


def _silu(x):
    return x * jax.nn.sigmoid(x)


def _modulate(x, g, shift, scale):
    ms = jnp.mean(x * x, axis=-1, keepdims=True)
    y = x * lax.rsqrt(ms + EPS) * g
    return y * (1.0 + scale) + shift


def _ada_kernel(c_ref, w_ref, b_ref, o_ref):
    ca = _silu(c_ref[...]).astype(BF16)
    o_ref[0] = _dot(ca, w_ref[0].astype(BF16)) + b_ref[0]


def _ada_mod(c, w_ada, b_ada):
    depth, d, nd = w_ada.shape
    b = c.shape[0]
    rows = 8
    c8 = jnp.zeros((rows, d), F32).at[:b].set(c)
    out = pl.pallas_call(
        _ada_kernel,
        grid=(depth, nd // d),
        in_specs=[pl.BlockSpec((rows, d), lambda l, j: (0, 0)),
                  pl.BlockSpec((1, d, d), lambda l, j: (l, 0, j)),
                  pl.BlockSpec((1, 1, d), lambda l, j: (l, 0, j))],
        out_specs=pl.BlockSpec((1, rows, d), lambda l, j: (l, 0, j)),
        out_shape=jax.ShapeDtypeStruct((depth, rows, nd), F32),
        name="ada_mod",
        compiler_params=_cparams(("arbitrary", "arbitrary")),
    )(c8, w_ada, b_ada.reshape(depth, 1, nd))
    return out[:, :b].reshape(depth, b, nd // d, d)


def _ffn_kernel(x_ref, mod_ref, g_ref, wg_ref, wu_ref, wd_ref, fg_ref, o_ref, *, mod_base, final):
    x = x_ref[0]
    m = mod_ref[0]
    shift, scale, gate = m[mod_base:mod_base + 1], m[mod_base + 1:mod_base + 2], m[mod_base + 2:mod_base + 3]
    h = _modulate(x, g_ref[0], shift, scale).astype(BF16)
    down = None
    for c0 in range(0, wg_ref.shape[2], FFN_CHUNK):
        a = _dot(h, wg_ref[0, :, c0:c0 + FFN_CHUNK].astype(BF16))
        u = _dot(h, wu_ref[0, :, c0:c0 + FFN_CHUNK].astype(BF16))
        act = (_silu(a) * u).astype(BF16)
        part = _dot(act, wd_ref[0, c0:c0 + FFN_CHUNK, :].astype(BF16))
        down = part if down is None else down + part
    y = x + (0.5 * gate) * down
    if final:
        y = y * lax.rsqrt(jnp.mean(y * y, axis=-1, keepdims=True) + EPS) * fg_ref[...]
    o_ref[0] = y


def _ffn(x, mod, g, wg, wu, wd, final_g, *, layer, mod_base, final, tm):
    b, s, d = x.shape
    depth, _, f = wg.shape
    resident = dict(pipeline_mode=pl.Buffered(1))
    return pl.pallas_call(
        functools.partial(_ffn_kernel, mod_base=mod_base, final=final),
        grid=(b, s // tm),
        in_specs=[pl.BlockSpec((1, tm, d), lambda bi, i: (bi, i, 0)),
                  pl.BlockSpec((1, N_MOD, d), lambda bi, i: (bi, 0, 0)),
                  pl.BlockSpec((1, 1, d), lambda bi, i: (layer, 0, 0)),
                  pl.BlockSpec((1, d, f), lambda bi, i: (layer, 0, 0), **resident),
                  pl.BlockSpec((1, d, f), lambda bi, i: (layer, 0, 0), **resident),
                  pl.BlockSpec((1, f, d), lambda bi, i: (layer, 0, 0), **resident),
                  pl.BlockSpec((1, d), lambda bi, i: (0, 0))],
        out_specs=pl.BlockSpec((1, tm, d), lambda bi, i: (bi, i, 0)),
        out_shape=jax.ShapeDtypeStruct((b, s, d), F32),
        name="ffn",
        compiler_params=_cparams(("arbitrary", "arbitrary")),
    )(x, mod, g.reshape(depth, 1, d), wg, wu, wd, final_g.reshape(1, d))


def _inproj_kernel(x_ref, mod_ref, g_ref, w_ref, o_ref):
    m = mod_ref[0]
    h = _modulate(x_ref[0], g_ref[...], m[3:4], m[4:5]).astype(BF16)
    o_ref[0] = _dot(h, w_ref[...])


def _inproj(x, mod, g, w, *, tm):
    b, s, d = x.shape
    n = w.shape[1]
    return pl.pallas_call(
        _inproj_kernel,
        grid=(b, s // tm),
        in_specs=[pl.BlockSpec((1, tm, d), lambda bi, i: (bi, i, 0)),
                  pl.BlockSpec((1, N_MOD, d), lambda bi, i: (bi, 0, 0)),
                  pl.BlockSpec((1, d), lambda bi, i: (0, 0)),
                  pl.BlockSpec((d, n), lambda bi, i: (0, 0), pipeline_mode=pl.Buffered(1))],
        out_specs=pl.BlockSpec((1, tm, n), lambda bi, i: (bi, i, 0)),
        out_shape=jax.ShapeDtypeStruct((b, s, n), F32),
        name="in_proj",
        compiler_params=_cparams(("arbitrary", "arbitrary")),
    )(x, mod, g.reshape(1, d), w)


def _conv_kernel(u_ref, p_ref, w_ref, b_ref, lg_ref, lb_ref, o_ref, hs_ref, *, ts, ch):
    i = pl.program_id(1)
    cur = u_ref[0]
    prev = p_ref[0]
    hprev = prev[:, :ch] * jax.nn.sigmoid(prev[:, ch:])
    hs_ref[0:CONV_HALO, :] = jnp.where(i > 0, hprev, 0.0)
    hs_ref[CONV_HALO:CONV_HALO + ts, :] = cur[:, :ch] * jax.nn.sigmoid(cur[:, ch:])
    base = CONV_HALO - (CONV_K - 1)
    acc = jnp.zeros((ts, ch), F32) + b_ref[...]
    for k in range(CONV_K):
        acc = acc + hs_ref[pl.ds(base + k, ts), :] * w_ref[k:k + 1, :]
    mu = jnp.mean(acc, axis=-1, keepdims=True)
    cen = acc - mu
    var = jnp.mean(cen * cen, axis=-1, keepdims=True)
    o_ref[0] = _silu(cen * lax.rsqrt(var + EPS) * lg_ref[...] + lb_ref[...])


def _conformer_conv(u, w, bias, ln_g, ln_b, *, ts):
    b, s, _ = u.shape
    ch = w.shape[1]
    hb = ts // CONV_HALO
    return pl.pallas_call(
        functools.partial(_conv_kernel, ts=ts, ch=ch),
        grid=(b, s // ts),
        in_specs=[pl.BlockSpec((1, ts, 2 * ch), lambda bi, i: (bi, i, 0)),
                  pl.BlockSpec((1, CONV_HALO, 2 * ch), lambda bi, i: (bi, jnp.maximum(i * hb - 1, 0), 0)),
                  pl.BlockSpec((CONV_K, ch), lambda bi, i: (0, 0)),
                  pl.BlockSpec((1, ch), lambda bi, i: (0, 0)),
                  pl.BlockSpec((1, ch), lambda bi, i: (0, 0)),
                  pl.BlockSpec((1, ch), lambda bi, i: (0, 0))],
        out_specs=pl.BlockSpec((1, ts, ch), lambda bi, i: (bi, i, 0)),
        out_shape=jax.ShapeDtypeStruct((b, s, ch), F32),
        scratch_shapes=[pltpu.VMEM((CONV_HALO + ts, ch), F32)],
        name="conformer_conv",
        compiler_params=_cparams(("arbitrary", "arbitrary")),
    )(u, u, w, bias.reshape(1, ch), ln_g.reshape(1, ch), ln_b.reshape(1, ch))


def _rope_tables(s):
    half = ROPE_DIM // 2
    inv = jnp.exp(-math.log(ROPE_THETA) * jnp.arange(0, ROPE_DIM, 2, dtype=F32) / ROPE_DIM)
    ang = jnp.arange(s, dtype=F32)[:, None] * inv[None, :]
    cs = jnp.concatenate([jnp.cos(ang), jnp.sin(ang), jnp.ones((s, 1), F32)], axis=-1)
    d = np.arange(LANES) % HEAD_DIM
    rows = 2 * half + 1
    mc, msa, msb = (np.zeros((rows, LANES), np.float32) for _ in range(3))
    for lane_i, di in enumerate(d):
        if di < ROPE_DIM:
            mc[di % half, lane_i] = 1.0
        else:
            mc[2 * half, lane_i] = 1.0
        if di < half:
            msa[half + di, lane_i] = -1.0
        elif di < ROPE_DIM:
            msb[half + (di - half), lane_i] = 1.0
    spread = lambda m: jnp.dot(cs, jnp.asarray(m), precision=lax.Precision.HIGHEST)
    return spread(mc), spread(msa), spread(msb)


def _mask_base(hh):
    return (1 - hh) * HEAD_DIM


def _moba_prep_kernel(q_ref, k_ref, v_ref, c_ref, sa_ref, sb_ref, qo_ref, ko_ref, vo_ref, km_ref):
    i = pl.program_id(2)
    half = ROPE_DIM // 2
    c, sa, sb = c_ref[...], sa_ref[...], sb_ref[...]

    def rope(x):
        return x * c + pltpu.roll(x, LANES - half, 1) * sa + pltpu.roll(x, half, 1) * sb

    qr, kr, v = rope(q_ref[0]), rope(k_ref[0]), v_ref[0]
    lane = _iota(qr.shape, 1)
    for hh in range(2):
        own = (lane >= hh * HEAD_DIM) & (lane < (hh + 1) * HEAD_DIM)
        mb = _mask_base(hh)
        qo_ref[0, hh] = jnp.where(own, qr, 0.0)
        ko_ref[0, hh] = jnp.where(own, kr, jnp.where(lane == mb + i, 1.0, 0.0)).astype(BF16)
        vo_ref[0, hh] = jnp.where(own, v, jnp.where(lane == mb, 1.0, 0.0)).astype(BF16)
        km_ref[0, hh, 0] = jnp.mean(jnp.where(own, kr, 0.0), axis=0, keepdims=True)


def _moba_prep(u, tables, *, q_lane_block):
    b, s, _ = u.shape
    nb = s // MOBA_BLOCK
    ts = MOBA_BLOCK
    c, sa, sb = tables
    qb = q_lane_block
    tab = pl.BlockSpec((ts, LANES), lambda bi, p, i: (i, 0))
    head = lambda dt: jax.ShapeDtypeStruct((b, 6, s, LANES), dt)
    return pl.pallas_call(
        _moba_prep_kernel,
        grid=(b, 3, nb),
        in_specs=[pl.BlockSpec((1, ts, LANES), lambda bi, p, i: (bi, i, qb + p)),
                  pl.BlockSpec((1, ts, LANES), lambda bi, p, i: (bi, i, qb + 3 + p)),
                  pl.BlockSpec((1, ts, LANES), lambda bi, p, i: (bi, i, qb + 6 + p)),
                  tab, tab, tab],
        out_specs=[pl.BlockSpec((1, 2, ts, LANES), lambda bi, p, i: (bi, p, i, 0)),
                   pl.BlockSpec((1, 2, ts, LANES), lambda bi, p, i: (bi, p, i, 0)),
                   pl.BlockSpec((1, 2, ts, LANES), lambda bi, p, i: (bi, p, i, 0)),
                   pl.BlockSpec((1, 2, 1, 1, LANES), lambda bi, p, i: (bi, p, i, 0, 0))],
        out_shape=[head(F32), head(BF16), head(BF16), jax.ShapeDtypeStruct((b, 6, nb, 1, LANES), F32)],
        name="moba_prep",
        compiler_params=_cparams(("arbitrary", "arbitrary", "arbitrary")),
    )(u, u, u, c, sa, sb)


def _block_mean_rows(km):
    b, h, nb = km.shape[:3]
    km = km[:, :, :, 0, :].reshape(b, h // 2, 2, nb, LANES)
    rows = [jnp.zeros((b, h // 2, LANES, LANES), F32).at[:, :, _mask_base(hh):_mask_base(hh) + nb, :].set(km[:, :, hh])
            for hh in range(2)]
    return jnp.stack(rows, axis=2).reshape(b, h, LANES, LANES)


def _moba_kernel(q_ref, ka_ref, va_ref, kmt_ref, o_ref):
    i = pl.program_id(2)
    blk = MOBA_BLOCK
    gk = MOBA_GROUP * blk
    lane = _iota((blk, LANES), 1)
    nsel = LANES // 4
    blk_id = _iota((nsel, blk), 0)
    blk_f = blk_id.astype(F32)
    neg_inf = -jnp.inf

    qas = []
    for hh in range(2):
        mb = _mask_base(hh)
        qp = q_ref[0, hh]
        gate = _dot_nt_split3(kmt_ref[0, hh], qp)[mb:mb + nsel, :]
        s = jnp.where(blk_id < i, gate, neg_inf)
        sel = jnp.where(blk_id == i, 1.0, 0.0)
        for _ in range(MOBA_TOPK):
            m = jnp.max(s, axis=0, keepdims=True)
            idx = jnp.min(jnp.where(s == m, blk_f, float(LANES)), axis=0, keepdims=True)
            hit = blk_f == idx
            sel = jnp.where(hit, 1.0, sel)
            s = jnp.where(hit, neg_inf, s)
        neg_t = jnp.where(sel > 0.5, 0.0, MASK_NEG)
        pad_lo = jnp.zeros((mb, blk), F32)
        pad_hi = jnp.zeros((LANES - mb - nsel, blk), F32)
        negm = jnp.concatenate([t for t in (pad_lo, neg_t, pad_hi) if t.shape[0]], axis=0).T
        qas.append((qp * (HEAD_DIM ** -0.5 * math.log2(math.e)) + negm).astype(BF16))

    def step(carry, start, causal):
        hs = range(2)
        s = [_dot_nt(qas[hh], ka_ref[0, hh, pl.ds(start, gk), :]) for hh in hs]
        if causal:
            keep = start + _iota((blk, gk), 1) <= i * blk + _iota((blk, gk), 0)
            s = [jnp.where(keep, t, MASK_NEG) for t in s]
        mn = [jnp.maximum(carry[2 * hh], jnp.max(s[hh], axis=-1, keepdims=True)) for hh in hs]
        p = [jnp.exp2(s[hh] - mn[hh]).astype(BF16) for hh in hs]
        pv = [_dot(p[hh], va_ref[0, hh, pl.ds(start, gk), :]) for hh in hs]
        out = []
        for hh in hs:
            out += [mn[hh], jnp.exp2(carry[2 * hh] - mn[hh]) * carry[2 * hh + 1] + pv[hh]]
        return tuple(out)

    m_init = jnp.full((blk, 1), neg_inf, F32)
    acc_init = jnp.zeros((blk, LANES), F32)
    n_past = i >> (MOBA_GROUP.bit_length() - 1)
    res = lax.fori_loop(0, n_past, lambda g, c: step(c, pl.multiple_of(g * gk, gk), False),
                        (m_init, acc_init, m_init, acc_init))
    res = step(res, pl.multiple_of(n_past * gk, gk), True)
    l0, l1 = _mask_base(0), _mask_base(1)
    o0 = res[1] / res[1][:, l0:l0 + 1]
    o1 = res[3] / res[3][:, l1:l1 + 1]
    o_ref[0, 0] = jnp.where(lane < HEAD_DIM, o0, o1)


def _moba_attention(qp, ka, va, kmt):
    b, _, s, _ = qp.shape
    nb = s // MOBA_BLOCK
    blk = MOBA_BLOCK
    return pl.pallas_call(
        _moba_kernel,
        grid=(b, 3, nb),
        in_specs=[pl.BlockSpec((1, 2, blk, LANES), lambda bi, p, i: (bi, p, i, 0)),
                  pl.BlockSpec((1, 2, s, LANES), lambda bi, p, i: (bi, p, 0, 0)),
                  pl.BlockSpec((1, 2, s, LANES), lambda bi, p, i: (bi, p, 0, 0)),
                  pl.BlockSpec((1, 2, LANES, LANES), lambda bi, p, i: (bi, p, 0, 0))],
        out_specs=pl.BlockSpec((1, 1, blk, LANES), lambda bi, p, i: (bi, p, i, 0)),
        out_shape=jax.ShapeDtypeStruct((b, 3, s, LANES), F32),
        name="moba_attention",
        compiler_params=_cparams(("arbitrary", "arbitrary", "arbitrary")),
    )(qp, ka, va, kmt)


def _gdn_prep_kernel(q_ref, k_ref, v_ref, hq_ref, hk_ref, hv_ref, ab_ref, wq_ref, wk_ref, wv_ref, al_ref, dt_ref,
                     qn_ref, kn_ref, kb_ref, vb_ref, kbg_ref, qg_ref, kd_ref, gc_ref, eg_ref, xs_ref, *, ts):
    i = pl.program_id(1)
    p = pl.program_id(2)
    base = GDN_HALO - (GDN_CONV_K - 1)

    def conv_silu(x_ref, h_ref, w_ref):
        xs_ref[0:GDN_HALO, :] = jnp.where(i > 0, h_ref[0], 0.0)
        xs_ref[GDN_HALO:GDN_HALO + ts, :] = x_ref[0]
        acc = jnp.zeros((ts, LANES), F32)
        for k in range(GDN_CONV_K):
            acc = acc + xs_ref[pl.ds(base + k, ts), :] * w_ref[k:k + 1, :]
        return _silu(acc)

    q, k, v = conv_silu(q_ref, hq_ref, wq_ref), conv_silu(k_ref, hk_ref, wk_ref), conv_silu(v_ref, hv_ref, wv_ref)

    r2 = _iota((2 * LANES, 2 * LANES), 0)
    c2 = _iota((2 * LANES, 2 * LANES), 1)
    head_sum = jnp.where(_div64(r2) == _div64(c2), 1.0, 0.0).astype(BF16)
    ssq = _dot_data_01(jnp.concatenate([q * q, k * k], axis=1), head_sum)
    qn = q * lax.rsqrt(ssq[:, :LANES] + EPS) * (HEAD_DIM ** -0.5)
    kn = k * lax.rsqrt(ssq[:, LANES:] + EPS)

    ab = ab_ref[0]
    lane = _iota(ab.shape, 1)
    nh = 6
    xa = ab + dt_ref[...]
    softplus = jnp.maximum(xa, 0.0) + jnp.log1p(jnp.exp(-jnp.abs(xa)))
    g_beta = jnp.where(lane < nh, -jnp.exp(al_ref[...]) * softplus,
                       jnp.where(lane < 2 * nh, jax.nn.sigmoid(ab), 0.0))
    pr = _iota((LANES, 2 * LANES), 0)
    pc = _iota((LANES, 2 * LANES), 1)
    pick = jnp.where(pr == 2 * p + _div64(pc & (LANES - 1)) + jnp.where(pc < LANES, 0, nh), 1.0, 0.0).astype(BF16)
    gbx = _dot_data_01(g_beta, pick)
    gx, bx = gbx[:, :LANES], gbx[:, LANES:]

    tr = _iota((ts, ts), 0)
    tc = _iota((ts, ts), 1)
    in_chunk_tril = jnp.where((_div64(tr) == _div64(tc)) & (tc <= tr), 1.0, 0.0).astype(BF16)
    gcx = _dot_01_data(in_chunk_tril, gx)
    nck = ts // GDN_CHUNK
    last = gcx.reshape(nck, GDN_CHUNK, LANES)[:, GDN_CHUNK - 1:GDN_CHUNK, :]
    glx = jnp.broadcast_to(last, (nck, GDN_CHUNK, LANES)).reshape(ts, LANES)
    eg = jnp.exp(gcx)

    kb = kn * bx
    qn_ref[0, 0] = qn.astype(BF16)
    kn_ref[0, 0] = kn.astype(BF16)
    kb_ref[0, 0] = kb.astype(BF16)
    vb_ref[0, 0] = v * bx
    kbg_ref[0, 0] = kb * eg
    qg_ref[0, 0] = (qn * eg).astype(BF16)
    kd_ref[0, 0] = kn * jnp.exp(glx - gcx)
    gc_ref[0, 0] = gcx
    eg_ref[0, 0] = jnp.exp(glx)


def _gdn_prep(u, conv_w, a_log, dt_bias, *, q_lane_block, ab_lane_block, ts):
    b, s, _ = u.shape
    qb = q_lane_block
    hb = ts // GDN_HALO
    nh = a_log.shape[0]
    pad = lambda t: jnp.zeros((1, LANES), F32).at[0, :nh].set(t)
    cur = lambda off: pl.BlockSpec((1, ts, LANES), lambda bi, i, p: (bi, i, qb + off + p))
    halo = lambda off: pl.BlockSpec((1, GDN_HALO, LANES), lambda bi, i, p: (bi, jnp.maximum(i * hb - 1, 0), qb + off + p))
    wsp = lambda off: pl.BlockSpec((GDN_CONV_K, LANES), lambda bi, i, p: (0, off + p))
    vec = pl.BlockSpec((1, LANES), lambda bi, i, p: (0, 0))
    osp = pl.BlockSpec((1, 1, ts, LANES), lambda bi, i, p: (bi, p, i, 0))
    sh = lambda dt: jax.ShapeDtypeStruct((b, 3, s, LANES), dt)
    return pl.pallas_call(
        functools.partial(_gdn_prep_kernel, ts=ts),
        grid=(b, s // ts, 3),
        in_specs=[cur(0), cur(3), cur(6), halo(0), halo(3), halo(6),
                  pl.BlockSpec((1, ts, LANES), lambda bi, i, p: (bi, i, ab_lane_block)),
                  wsp(0), wsp(3), wsp(6), vec, vec],
        out_specs=[osp] * 9,
        out_shape=[sh(BF16), sh(BF16), sh(BF16), sh(F32), sh(F32), sh(BF16), sh(F32), sh(F32), sh(F32)],
        scratch_shapes=[pltpu.VMEM((GDN_HALO + ts, LANES), F32)],
        name="gdn_prep",
        compiler_params=_cparams(("arbitrary", "arbitrary", "arbitrary")),
    )(u, u, u, u, u, u, u, conv_w, conv_w, conv_w, pad(a_log), pad(dt_bias))


def _stack2(top, bot):
    return jnp.concatenate([top, bot], axis=0)


def _gdn_solve_kernel(qn_ref, kn_ref, kb_ref, vb_ref, kbg_ref, kd_ref, gc_ref, x_ref, qk_ref, kdt_ref, *, chunks):
    cs = GDN_CHUNK
    n2 = 2 * cs
    lane = _iota((cs, LANES), 1)
    low = lane < HEAD_DIM
    r = _iota((n2, n2), 0)
    c = _iota((n2, n2), 1)
    bd = _div64(r) == _div64(c)
    eye = jnp.where(r == c, 1.0, 0.0)
    diag8 = (r >> 3) == (c >> 3)
    merge_masks = [((r >> (sh + 1)) == (c >> (sh + 1))) & (((r >> sh) & 1) == 1) & (((c >> sh) & 1) == 0)
                   for sh in (3, 4, 5)]

    def per_head_rows(t, fill=0.0):
        return _stack2(jnp.where(low, t, fill), jnp.where(low, fill, t))

    cis = range(chunks)
    rows = [pl.ds(ci * cs, cs) for ci in cis]
    rows2 = [pl.ds(ci * n2, n2) for ci in cis]
    kn = [per_head_rows(kn_ref[0, 0, rw, :]) for rw in rows]
    kb = [per_head_rows(kb_ref[0, 0, rw, :]) for rw in rows]
    qn = [per_head_rows(qn_ref[0, 0, rw, :]) for rw in rows]
    kk = [_dot_nt(kb[ci], kn[ci]) for ci in cis]
    qk = [_dot_nt(qn[ci], kn[ci]) for ci in cis]
    decay = []
    for rw in rows:
        gc = gc_ref[0, 0, rw, :]
        gsw = pltpu.roll(gc, HEAD_DIM, 1)
        gcol = _stack2(jnp.where(low, gc, gsw), jnp.where(low, gsw, gc))
        decay.append(jnp.exp(jnp.minimum(gcol - gcol.T, 0.0)))
    lmat = [jnp.where(c < r, kk[ci] * decay[ci], 0.0) for ci in cis]
    for ci in cis:
        qk_ref[0, 0, rows2[ci], :] = jnp.where(c <= r, qk[ci] * decay[ci], 0.0).astype(BF16)

    ld = [jnp.where(diag8, lm, 0.0) for lm in lmat]
    ld2 = [_dot_split2(m, m) for m in ld]
    ld4 = [_dot_split2(m, m) for m in ld2]
    t = [eye - m for m in ld]
    t = [t[ci] + _dot_split2(t[ci], ld2[ci]) for ci in cis]
    t = [t[ci] + _dot_split2(t[ci], ld4[ci]) for ci in cis]
    for lower_left in merge_masks:
        tb = [m.astype(BF16) for m in t]
        inner = [_dot(jnp.where(lower_left, lmat[ci], 0.0).astype(BF16), tb[ci]) for ci in cis]
        t = [t[ci] - _dot(tb[ci], inner[ci].astype(BF16)) for ci in cis]
    for ci in cis:
        kbg = kbg_ref[0, 0, rows[ci], :]
        vsw = pltpu.roll(vb_ref[0, 0, rows[ci], :], HEAD_DIM, 1)
        x = jnp.where(bd, _stack2(kbg, kbg), _stack2(vsw, vsw))
        x_ref[0, 0, rows2[ci], :] = _dot(t[ci].astype(BF16), x.astype(BF16))
    for ci in cis:
        kd = kd_ref[0, 0, rows[ci], :]
        kdt_ref[0, 0, rows2[ci], :] = jnp.where(bd, _stack2(kd, kd), 0.0).T.astype(BF16)


def _gdn_solve(qn, kn, kb, vb, kbg, kd, gc, *, ts):
    b, _, s, _ = qn.shape
    chunks = ts // GDN_CHUNK
    isp = pl.BlockSpec((1, 1, ts, LANES), lambda bi, p, i: (bi, p, i, 0))
    osp = pl.BlockSpec((1, 1, 2 * ts, LANES), lambda bi, p, i: (bi, p, i, 0))
    sh = lambda dt: jax.ShapeDtypeStruct((b, 3, 2 * s, LANES), dt)
    return pl.pallas_call(
        functools.partial(_gdn_solve_kernel, chunks=chunks),
        grid=(b, 3, s // ts),
        in_specs=[isp] * 7,
        out_specs=[osp] * 3,
        out_shape=[sh(F32), sh(BF16), sh(BF16)],
        name="gdn_solve",
        compiler_params=_cparams(("arbitrary", "arbitrary", "arbitrary")),
    )(qn, kn, kb, vb, kbg, kd, gc)


def _gdn_scan_kernel(x_ref, qk_ref, kdt_ref, qg_ref, eg_ref, z0_ref, z1_ref, z2_ref, ng_ref, o_ref, st_ref,
                     *, chunks, batch):
    cs = GDN_CHUNK
    n2 = 2 * cs
    r = _iota((n2, n2), 0)
    c = _iota((n2, n2), 1)
    bd = _div64(r) == _div64(c)
    hr = _iota((LANES, LANES), 0)
    hc = _iota((LANES, LANES), 1)
    head_mean = jnp.where(_div64(hr) == _div64(hc), 1.0 / HEAD_DIM, 0.0).astype(BF16)
    z_refs = (z0_ref, z1_ref, z2_ref)

    @pl.when(pl.program_id(0) == 0)
    def _():
        st_ref[...] = jnp.zeros_like(st_ref)

    def body(ci, carry):
        start = pl.multiple_of(ci * cs, cs)
        rows = pl.ds(start, cs)
        rows2 = pl.ds(pl.multiple_of(ci * n2, n2), n2)
        chains = [(bi, p) for bi in range(batch) for p in range(3)]
        ks = range(len(chains))
        st = [st_ref[bi * 3 + p] for bi, p in chains]
        stb = [t.astype(BF16) for t in st]
        x = [x_ref[bi, p, rows2, :] for bi, p in chains]
        ws = [_dot(jnp.where(bd, x[k], 0.0).astype(BF16), stb[k]) for k in ks]
        vnb = [(jnp.where(bd, 0.0, x[k]) - ws[k]).astype(BF16) for k in ks]
        qbd = [jnp.where(bd, _stack2(qg, qg), jnp.zeros((), BF16)) for qg in (qg_ref[bi, p, rows, :] for bi, p in chains)]
        o = [_dot(qbd[k], stb[k]) + _dot(qk_ref[bi, p, rows2, :], vnb[k]) for k, (bi, p) in enumerate(chains)]
        upd = [_dot(kdt_ref[bi, p, rows2, :], vnb[k]) for k, (bi, p) in enumerate(chains)]
        for k, (bi, p) in enumerate(chains):
            eg = pltpu.roll(eg_ref[bi, p, pl.ds(start, 8), :], HEAD_DIM, 1)[0:1, :]
            st_ref[bi * 3 + p] = st[k] * eg + upd[k]
        op = [pltpu.roll(t[:cs] + t[cs:], HEAD_DIM, 1) for t in o]
        ms = [_dot_data_01(t * t, head_mean) for t in op]
        for k, (bi, p) in enumerate(chains):
            z = z_refs[p][bi, rows, :]
            o_ref[bi, p, rows, :] = op[k] * lax.rsqrt(ms[k] + EPS) * ng_ref[...] * _silu(z)
        return carry

    lax.fori_loop(0, chunks, body, 0)


def _gdn_scan(x, qk, kdt, qg, eg, u, norm_g, *, z_lane_block, ts):
    b, _, s, _ = qg.shape
    chunks = ts // GDN_CHUNK
    zb = z_lane_block
    two = pl.BlockSpec((b, 3, 2 * ts, LANES), lambda i: (0, 0, i, 0))
    one = pl.BlockSpec((b, 3, ts, LANES), lambda i: (0, 0, i, 0))
    zsp = lambda p: pl.BlockSpec((b, ts, LANES), lambda i: (0, i, zb + p))
    ng = jnp.concatenate([norm_g, norm_g]).reshape(1, LANES)
    return pl.pallas_call(
        functools.partial(_gdn_scan_kernel, chunks=chunks, batch=b),
        grid=(s // ts,),
        in_specs=[two, two, two, one, one, zsp(0), zsp(1), zsp(2), pl.BlockSpec((1, LANES), lambda i: (0, 0))],
        out_specs=one,
        out_shape=jax.ShapeDtypeStruct((b, 3, s, LANES), F32),
        scratch_shapes=[pltpu.VMEM((b * 3, LANES, LANES), F32)],
        name="gdn_scan",
        compiler_params=_cparams(("arbitrary",)),
    )(x, qk, kdt, qg, eg, u, u, u, ng)


def _outproj_kernel(x_ref, mod_ref, yc_ref, ym_ref, yg_ref, w_ref, o_ref, *, conv_ch):
    acc = _dot(yc_ref[0].astype(BF16), w_ref[0:conv_ch, :])
    off = conv_ch
    for y_ref in (ym_ref, yg_ref):
        for p in range(3):
            acc = acc + _dot(y_ref[0, p].astype(BF16), w_ref[off:off + LANES, :])
            off += LANES
    o_ref[0] = x_ref[0] + mod_ref[0][5:6] * acc


def _outproj(x, mod, yc, ym, yg, w, *, tm):
    b, s, d = x.shape
    ch = yc.shape[-1]
    return pl.pallas_call(
        functools.partial(_outproj_kernel, conv_ch=ch),
        grid=(b, s // tm),
        in_specs=[pl.BlockSpec((1, tm, d), lambda bi, i: (bi, i, 0)),
                  pl.BlockSpec((1, N_MOD, d), lambda bi, i: (bi, 0, 0)),
                  pl.BlockSpec((1, tm, ch), lambda bi, i: (bi, i, 0)),
                  pl.BlockSpec((1, 3, tm, LANES), lambda bi, i: (bi, 0, i, 0)),
                  pl.BlockSpec((1, 3, tm, LANES), lambda bi, i: (bi, 0, i, 0)),
                  pl.BlockSpec(w.shape, lambda bi, i: (0, 0), pipeline_mode=pl.Buffered(1))],
        out_specs=pl.BlockSpec((1, tm, d), lambda bi, i: (bi, i, 0)),
        out_shape=jax.ShapeDtypeStruct((b, s, d), F32),
        name="out_proj",
        compiler_params=_cparams(("arbitrary", "arbitrary")),
    )(x, mod, yc, ym, yg, w)


def _token_mixing(x, mod, ln_g, w_in, conv_w, conv_b, conv_ln_g, conv_ln_b, gdn_conv_w, gdn_a_log, gdn_dt_bias,
                  gdn_norm_g, w_out, tables, *, tm, ts_conv, ts_gdn, ts_solve, ts_scan):
    d, d_in = w_in.shape
    conv_ch = conv_w.shape[1]
    n_pad = -(-d_in // LANES) * LANES
    w_in_p = jnp.zeros((d, n_pad), BF16).at[:, :d_in].set(w_in.astype(BF16))
    u = _inproj(x, mod, ln_g, w_in_p, tm=tm)

    moba_w = 3 * LANES
    q_blk_moba = 2 * conv_ch // LANES
    q_blk_gdn = q_blk_moba + 3 * moba_w // LANES
    z_blk = q_blk_gdn + 9
    ab_blk = z_blk + 3

    y_conv = _conformer_conv(u, conv_w, conv_b, conv_ln_g, conv_ln_b, ts=ts_conv)

    qp, ka, va, km = _moba_prep(u, tables, q_lane_block=q_blk_moba)
    y_moba = _moba_attention(qp, ka, va, _block_mean_rows(km))

    qn, kn, kb, vb, kbg, qg, kd, gc, eg = _gdn_prep(u, gdn_conv_w, gdn_a_log, gdn_dt_bias, q_lane_block=q_blk_gdn,
                                                     ab_lane_block=ab_blk, ts=ts_gdn)
    xs, qk, kdt = _gdn_solve(qn, kn, kb, vb, kbg, kd, gc, ts=ts_solve)
    y_gdn = _gdn_scan(xs, qk, kdt, qg, eg, u, gdn_norm_g, z_lane_block=z_blk, ts=ts_scan)

    return _outproj(x, mod, y_conv, y_moba, y_gdn, w_out.astype(BF16), tm=tm)


def kernel(x, c, w_ada, b_ada, ln_ffn1_g, ffn1_w_gate, ffn1_w_up, ffn1_w_down, ln_mix_g, w_in, conv_w, conv_b, conv_ln_g, conv_ln_b, gdn_conv_w, gdn_a_log, gdn_dt_bias, gdn_norm_g, w_out, ln_ffn2_g, ffn2_w_gate, ffn2_w_up, ffn2_w_down, final_g):
    depth = w_ada.shape[0]
    s = x.shape[1]
    tm = min(512, s)
    tiles = dict(tm=tm, ts_conv=min(512, s), ts_gdn=min(256, s), ts_solve=min(512, s), ts_scan=min(512, s))
    mods = _ada_mod(c, w_ada, b_ada)
    tables = _rope_tables(s)
    for l in range(depth):
        mod = mods[l]
        x = _ffn(x, mod, ln_ffn1_g, ffn1_w_gate, ffn1_w_up, ffn1_w_down, final_g,
                 layer=l, mod_base=0, final=False, tm=tm)
        x = _token_mixing(x, mod, ln_mix_g[l], w_in[l], conv_w[l], conv_b[l], conv_ln_g[l], conv_ln_b[l],
                          gdn_conv_w[l], gdn_a_log[l], gdn_dt_bias[l], gdn_norm_g[l], w_out[l], tables, **tiles)
        x = _ffn(x, mod, ln_ffn2_g, ffn2_w_gate, ffn2_w_up, ffn2_w_down, final_g,
                 layer=l, mod_base=6, final=(l == depth - 1), tm=tm)
    return x
```

```python
import functools
import math

import numpy as np
import jax
import jax.numpy as jnp
from jax import lax
from jax.experimental import pallas as pl
from jax.experimental.pallas import tpu as pltpu

F32 = jnp.float32
BF16 = jnp.bfloat16

EPS = 1e-6
HEAD_DIM = 64
LANES = 128
SUBLANES = 8
N_MOD = 9
FFN_CHUNK = 256
CONV_K = 31
CONV_HALO = 32
MOBA_BLOCK = 256
MOBA_TOPK = 3
MOBA_GROUP = 4
ROPE_DIM = HEAD_DIM // 4
ROPE_THETA = 500000.0
GDN_CONV_K = 4
GDN_HALO = 8
GDN_CHUNK = 64
MASK_NEG = -(2.0 ** 100)
VMEM_LIMIT = 56 * 1024 * 1024


def _cparams(sem):
    return pltpu.CompilerParams(dimension_semantics=sem, vmem_limit_bytes=VMEM_LIMIT)


def _iota(shape, dim):
    return lax.broadcasted_iota(jnp.int32, shape, dim)


def _div64(t):
    return t >> 6


def _dot(a, b):
    return jnp.dot(a, b, preferred_element_type=F32)


def _dot_nt(a, b):
    return lax.dot_general(a, b, (((1,), (1,)), ((), ())), preferred_element_type=F32)


def _split3(x):
    x0 = x.astype(BF16)
    r1 = x - x0.astype(F32)
    x1 = r1.astype(BF16)
    x2 = (r1 - x1.astype(F32)).astype(BF16)
    return x0, x1, x2


def _dot_nt_split3(a, b):
    a0, a1, a2 = _split3(a)
    b0, b1, b2 = _split3(b)
    small = _dot_nt(a0, b2) + _dot_nt(a1, b1) + _dot_nt(a2, b0)
    mid = _dot_nt(a0, b1) + _dot_nt(a1, b0)
    return _dot_nt(a0, b0) + (mid + small)


def _dot_data_01(x, m01):
    x0, x1, x2 = _split3(x)
    return _dot(x0, m01) + (_dot(x1, m01) + _dot(x2, m01))


def _dot_01_data(m01, x):
    x0, x1, x2 = _split3(x)
    return _dot(m01, x0) + (_dot(m01, x1) + _dot(m01, x2))


def _dot_split2(a, b):
    a0 = a.astype(BF16)
    a1 = (a - a0.astype(F32)).astype(BF16)
    b0 = b.astype(BF16)
    b1 = (b - b0.astype(F32)).astype(BF16)
    return _dot(a0, b0) + (_dot(a0, b1) + _dot(a1, b0))


def _silu(x):
    return x * jax.nn.sigmoid(x)


def _modulate(x, g, shift, scale):
    ms = jnp.mean(x * x, axis=-1, keepdims=True)
    y = x * lax.rsqrt(ms + EPS) * g
    return y * (1.0 + scale) + shift


def _ada_kernel(c_ref, w_ref, b_ref, o_ref):
    ca = _silu(c_ref[...]).astype(BF16)
    o_ref[0] = _dot(ca, w_ref[0].astype(BF16)) + b_ref[0]


def _ada_mod(c, w_ada, b_ada):
    depth, d, nd = w_ada.shape
    b = c.shape[0]
    rows = 8
    c8 = jnp.zeros((rows, d), F32).at[:b].set(c)
    out = pl.pallas_call(
        _ada_kernel,
        grid=(depth, nd // d),
        in_specs=[pl.BlockSpec((rows, d), lambda l, j: (0, 0)),
                  pl.BlockSpec((1, d, d), lambda l, j: (l, 0, j)),
                  pl.BlockSpec((1, 1, d), lambda l, j: (l, 0, j))],
        out_specs=pl.BlockSpec((1, rows, d), lambda l, j: (l, 0, j)),
        out_shape=jax.ShapeDtypeStruct((depth, rows, nd), F32),
        name="ada_mod",
        compiler_params=_cparams(("arbitrary", "arbitrary")),
    )(c8, w_ada, b_ada.reshape(depth, 1, nd))
    return out[:, :b].reshape(depth, b, nd // d, d)


def _ffn_kernel(x_ref, mod_ref, g_ref, wg_ref, wu_ref, wd_ref, fg_ref, o_ref, *, mod_base, final):
    x = x_ref[0]
    m = mod_ref[0]
    shift, scale, gate = m[mod_base:mod_base + 1], m[mod_base + 1:mod_base + 2], m[mod_base + 2:mod_base + 3]
    h = _modulate(x, g_ref[0], shift, scale).astype(BF16)
    down = None
    for c0 in range(0, wg_ref.shape[2], FFN_CHUNK):
        a = _dot(h, wg_ref[0, :, c0:c0 + FFN_CHUNK].astype(BF16))
        u = _dot(h, wu_ref[0, :, c0:c0 + FFN_CHUNK].astype(BF16))
        act = (_silu(a) * u).astype(BF16)
        part = _dot(act, wd_ref[0, c0:c0 + FFN_CHUNK, :].astype(BF16))
        down = part if down is None else down + part
    y = x + (0.5 * gate) * down
    if final:
        y = y * lax.rsqrt(jnp.mean(y * y, axis=-1, keepdims=True) + EPS) * fg_ref[...]
    o_ref[0] = y


def _ffn(x, mod, g, wg, wu, wd, final_g, *, layer, mod_base, final, tm):
    b, s, d = x.shape
    depth, _, f = wg.shape
    resident = dict(pipeline_mode=pl.Buffered(1))
    return pl.pallas_call(
        functools.partial(_ffn_kernel, mod_base=mod_base, final=final),
        grid=(b, s // tm),
        in_specs=[pl.BlockSpec((1, tm, d), lambda bi, i: (bi, i, 0)),
                  pl.BlockSpec((1, N_MOD, d), lambda bi, i: (bi, 0, 0)),
                  pl.BlockSpec((1, 1, d), lambda bi, i: (layer, 0, 0)),
                  pl.BlockSpec((1, d, f), lambda bi, i: (layer, 0, 0), **resident),
                  pl.BlockSpec((1, d, f), lambda bi, i: (layer, 0, 0), **resident),
                  pl.BlockSpec((1, f, d), lambda bi, i: (layer, 0, 0), **resident),
                  pl.BlockSpec((1, d), lambda bi, i: (0, 0))],
        out_specs=pl.BlockSpec((1, tm, d), lambda bi, i: (bi, i, 0)),
        out_shape=jax.ShapeDtypeStruct((b, s, d), F32),
        name="ffn",
        compiler_params=_cparams(("arbitrary", "arbitrary")),
    )(x, mod, g.reshape(depth, 1, d), wg, wu, wd, final_g.reshape(1, d))


def _inproj_kernel(x_ref, mod_ref, g_ref, w_ref, o_ref):
    m = mod_ref[0]
    h = _modulate(x_ref[0], g_ref[...], m[3:4], m[4:5]).astype(BF16)
    o_ref[0] = _dot(h, w_ref[...])


def _inproj(x, mod, g, w, *, tm):
    b, s, d = x.shape
    n = w.shape[1]
    return pl.pallas_call(
        _inproj_kernel,
        grid=(b, s // tm),
        in_specs=[pl.BlockSpec((1, tm, d), lambda bi, i: (bi, i, 0)),
                  pl.BlockSpec((1, N_MOD, d), lambda bi, i: (bi, 0, 0)),
                  pl.BlockSpec((1, d), lambda bi, i: (0, 0)),
                  pl.BlockSpec((d, n), lambda bi, i: (0, 0), pipeline_mode=pl.Buffered(1))],
        out_specs=pl.BlockSpec((1, tm, n), lambda bi, i: (bi, i, 0)),
        out_shape=jax.ShapeDtypeStruct((b, s, n), F32),
        name="in_proj",
        compiler_params=_cparams(("arbitrary", "arbitrary")),
    )(x, mod, g.reshape(1, d), w)


def _conv_kernel(u_ref, p_ref, w_ref, b_ref, lg_ref, lb_ref, o_ref, hs_ref, sh_ref, *, ts, ch):
    i = pl.program_id(1)
    cur = u_ref[0]
    prev = p_ref[0]
    hprev = prev[:, :ch] * jax.nn.sigmoid(prev[:, ch:])
    hs_ref[0:CONV_HALO, :] = jnp.where(i > 0, hprev, 0.0)
    hs_ref[CONV_HALO:CONV_HALO + ts, :] = cur[:, :ch] * jax.nn.sigmoid(cur[:, ch:])
    hs_ref[CONV_HALO + ts:CONV_HALO + ts + SUBLANES, :] = jnp.zeros((SUBLANES, ch), F32)
    span = CONV_HALO + ts
    for r in range(SUBLANES):
        sh_ref[r] = hs_ref[pl.ds(r, span), :]
    base = CONV_HALO - (CONV_K - 1)
    acc = jnp.zeros((ts, ch), F32) + b_ref[...]
    for k in range(CONV_K):
        r = (base + k) % SUBLANES
        acc = acc + sh_ref[r, pl.ds(base + k - r, ts), :] * w_ref[k:k + 1, :]
    mu = jnp.mean(acc, axis=-1, keepdims=True)
    cen = acc - mu
    var = jnp.mean(cen * cen, axis=-1, keepdims=True)
    o_ref[0] = _silu(cen * lax.rsqrt(var + EPS) * lg_ref[...] + lb_ref[...])


def _conformer_conv(u, w, bias, ln_g, ln_b, *, ts):
    b, s, _ = u.shape
    ch = w.shape[1]
    hb = ts // CONV_HALO
    return pl.pallas_call(
        functools.partial(_conv_kernel, ts=ts, ch=ch),
        grid=(b, s // ts),
        in_specs=[pl.BlockSpec((1, ts, 2 * ch), lambda bi, i: (bi, i, 0)),
                  pl.BlockSpec((1, CONV_HALO, 2 * ch), lambda bi, i: (bi, jnp.maximum(i * hb - 1, 0), 0)),
                  pl.BlockSpec((CONV_K, ch), lambda bi, i: (0, 0)),
                  pl.BlockSpec((1, ch), lambda bi, i: (0, 0)),
                  pl.BlockSpec((1, ch), lambda bi, i: (0, 0)),
                  pl.BlockSpec((1, ch), lambda bi, i: (0, 0))],
        out_specs=pl.BlockSpec((1, ts, ch), lambda bi, i: (bi, i, 0)),
        out_shape=jax.ShapeDtypeStruct((b, s, ch), F32),
        scratch_shapes=[pltpu.VMEM((CONV_HALO + ts + SUBLANES, ch), F32),
                        pltpu.VMEM((SUBLANES, CONV_HALO + ts, ch), F32)],
        name="conformer_conv",
        compiler_params=_cparams(("arbitrary", "arbitrary")),
    )(u, u, w, bias.reshape(1, ch), ln_g.reshape(1, ch), ln_b.reshape(1, ch))


def _rope_tables(s):
    half = ROPE_DIM // 2
    inv = jnp.exp(-math.log(ROPE_THETA) * jnp.arange(0, ROPE_DIM, 2, dtype=F32) / ROPE_DIM)
    ang = jnp.arange(s, dtype=F32)[:, None] * inv[None, :]
    cs = jnp.concatenate([jnp.cos(ang), jnp.sin(ang), jnp.ones((s, 1), F32)], axis=-1)
    d = np.arange(LANES) % HEAD_DIM
    rows = 2 * half + 1
    mc, msa, msb = (np.zeros((rows, LANES), np.float32) for _ in range(3))
    for lane_i, di in enumerate(d):
        if di < ROPE_DIM:
            mc[di % half, lane_i] = 1.0
        else:
            mc[2 * half, lane_i] = 1.0
        if di < half:
            msa[half + di, lane_i] = -1.0
        elif di < ROPE_DIM:
            msb[half + (di - half), lane_i] = 1.0
    spread = lambda m: jnp.dot(cs, jnp.asarray(m), precision=lax.Precision.HIGHEST)
    return spread(mc), spread(msa), spread(msb)


def _mask_base(hh):
    return (1 - hh) * HEAD_DIM


def _moba_prep_kernel(q_ref, k_ref, v_ref, c_ref, sa_ref, sb_ref, qo_ref, ko_ref, vo_ref, km_ref):
    i = pl.program_id(2)
    half = ROPE_DIM // 2
    c, sa, sb = c_ref[...], sa_ref[...], sb_ref[...]

    def rope(x):
        return x * c + pltpu.roll(x, LANES - half, 1) * sa + pltpu.roll(x, half, 1) * sb

    qr, kr, v = rope(q_ref[0]), rope(k_ref[0]), v_ref[0]
    lane = _iota(qr.shape, 1)
    nbs = qr.shape[0] // MOBA_BLOCK
    blk_of_row = i * nbs + (_iota(qr.shape, 0) >> (MOBA_BLOCK.bit_length() - 1))
    for hh in range(2):
        own = (lane >= hh * HEAD_DIM) & (lane < (hh + 1) * HEAD_DIM)
        mb = _mask_base(hh)
        kz = jnp.where(own, kr, 0.0)
        qo_ref[0, hh] = jnp.where(own, qr, 0.0)
        ko_ref[0, hh] = jnp.where(own, kr, jnp.where(lane == mb + blk_of_row, 1.0, 0.0)).astype(BF16)
        vo_ref[0, hh] = jnp.where(own, v, jnp.where(lane == mb, 1.0, 0.0)).astype(BF16)
        for j in range(nbs):
            km_ref[0, hh, j] = jnp.mean(kz[j * MOBA_BLOCK:(j + 1) * MOBA_BLOCK], axis=0, keepdims=True)


def _moba_prep(u, tables, *, q_lane_block, ts):
    b, s, _ = u.shape
    nb = s // MOBA_BLOCK
    nbs = ts // MOBA_BLOCK
    c, sa, sb = tables
    qb = q_lane_block
    tab = pl.BlockSpec((ts, LANES), lambda bi, p, i: (i, 0))
    head = lambda dt: jax.ShapeDtypeStruct((b, 6, s, LANES), dt)
    return pl.pallas_call(
        _moba_prep_kernel,
        grid=(b, 3, s // ts),
        in_specs=[pl.BlockSpec((1, ts, LANES), lambda bi, p, i: (bi, i, qb + p)),
                  pl.BlockSpec((1, ts, LANES), lambda bi, p, i: (bi, i, qb + 3 + p)),
                  pl.BlockSpec((1, ts, LANES), lambda bi, p, i: (bi, i, qb + 6 + p)),
                  tab, tab, tab],
        out_specs=[pl.BlockSpec((1, 2, ts, LANES), lambda bi, p, i: (bi, p, i, 0)),
                   pl.BlockSpec((1, 2, ts, LANES), lambda bi, p, i: (bi, p, i, 0)),
                   pl.BlockSpec((1, 2, ts, LANES), lambda bi, p, i: (bi, p, i, 0)),
                   pl.BlockSpec((1, 2, nbs, 1, LANES), lambda bi, p, i: (bi, p, i, 0, 0))],
        out_shape=[head(F32), head(BF16), head(BF16), jax.ShapeDtypeStruct((b, 6, nb, 1, LANES), F32)],
        name="moba_prep",
        compiler_params=_cparams(("arbitrary", "arbitrary", "arbitrary")),
    )(u, u, u, c, sa, sb)


def _block_mean_rows(km):
    b, h, nb = km.shape[:3]
    km = km[:, :, :, 0, :].reshape(b, h // 2, 2, nb, LANES)
    rows = [jnp.zeros((b, h // 2, LANES, LANES), F32).at[:, :, _mask_base(hh):_mask_base(hh) + nb, :].set(km[:, :, hh])
            for hh in range(2)]
    return jnp.stack(rows, axis=2).reshape(b, h, LANES, LANES)


def _moba_kernel(q_ref, ka_ref, va_ref, kmt_ref, o_ref):
    i = pl.program_id(2)
    blk = MOBA_BLOCK
    gk = MOBA_GROUP * blk
    lane = _iota((blk, LANES), 1)
    nsel = LANES // 4
    blk_id = _iota((nsel, blk), 0)
    blk_f = blk_id.astype(F32)
    neg_inf = -jnp.inf

    qas = []
    for hh in range(2):
        mb = _mask_base(hh)
        qp = q_ref[0, hh]
        gate = _dot_nt_split3(kmt_ref[0, hh], qp)[mb:mb + nsel, :]
        s = jnp.where(blk_id < i, gate, neg_inf)
        sel = jnp.where(blk_id == i, 1.0, 0.0)
        for _ in range(MOBA_TOPK):
            m = jnp.max(s, axis=0, keepdims=True)
            idx = jnp.min(jnp.where(s == m, blk_f, float(LANES)), axis=0, keepdims=True)
            hit = blk_f == idx
            sel = jnp.where(hit, 1.0, sel)
            s = jnp.where(hit, neg_inf, s)
        neg_t = jnp.where(sel > 0.5, 0.0, MASK_NEG)
        pad_lo = jnp.zeros((mb, blk), F32)
        pad_hi = jnp.zeros((LANES - mb - nsel, blk), F32)
        negm = jnp.concatenate([t for t in (pad_lo, neg_t, pad_hi) if t.shape[0]], axis=0).T
        qas.append((qp * (HEAD_DIM ** -0.5 * math.log2(math.e)) + negm).astype(BF16))

    def step(carry, start, causal):
        hs = range(2)
        s = [_dot_nt(qas[hh], ka_ref[0, hh, pl.ds(start, gk), :]) for hh in hs]
        if causal:
            keep = start + _iota((blk, gk), 1) <= i * blk + _iota((blk, gk), 0)
            s = [jnp.where(keep, t, MASK_NEG) for t in s]
        mn = [jnp.maximum(carry[2 * hh], jnp.max(s[hh], axis=-1, keepdims=True)) for hh in hs]
        p = [jnp.exp2(s[hh] - mn[hh]).astype(BF16) for hh in hs]
        pv = [_dot(p[hh], va_ref[0, hh, pl.ds(start, gk), :]) for hh in hs]
        out = []
        for hh in hs:
            out += [mn[hh], jnp.exp2(carry[2 * hh] - mn[hh]) * carry[2 * hh + 1] + pv[hh]]
        return tuple(out)

    m_init = jnp.full((blk, 1), neg_inf, F32)
    acc_init = jnp.zeros((blk, LANES), F32)
    n_past = i >> (MOBA_GROUP.bit_length() - 1)
    res = lax.fori_loop(0, n_past, lambda g, c: step(c, pl.multiple_of(g * gk, gk), False),
                        (m_init, acc_init, m_init, acc_init))
    res = step(res, pl.multiple_of(n_past * gk, gk), True)
    l0, l1 = _mask_base(0), _mask_base(1)
    o0 = res[1] / res[1][:, l0:l0 + 1]
    o1 = res[3] / res[3][:, l1:l1 + 1]
    o_ref[0, 0] = jnp.where(lane < HEAD_DIM, o0, o1)


def _moba_attention(qp, ka, va, kmt):
    b, _, s, _ = qp.shape
    nb = s // MOBA_BLOCK
    blk = MOBA_BLOCK
    return pl.pallas_call(
        _moba_kernel,
        grid=(b, 3, nb),
        in_specs=[pl.BlockSpec((1, 2, blk, LANES), lambda bi, p, i: (bi, p, i, 0)),
                  pl.BlockSpec((1, 2, s, LANES), lambda bi, p, i: (bi, p, 0, 0)),
                  pl.BlockSpec((1, 2, s, LANES), lambda bi, p, i: (bi, p, 0, 0)),
                  pl.BlockSpec((1, 2, LANES, LANES), lambda bi, p, i: (bi, p, 0, 0))],
        out_specs=pl.BlockSpec((1, 1, blk, LANES), lambda bi, p, i: (bi, p, i, 0)),
        out_shape=jax.ShapeDtypeStruct((b, 3, s, LANES), F32),
        name="moba_attention",
        compiler_params=_cparams(("arbitrary", "arbitrary", "arbitrary")),
    )(qp, ka, va, kmt)


def _gdn_chunks_kernel(q_ref, k_ref, v_ref, hq_ref, hk_ref, hv_ref, ab_ref, wq_ref, wk_ref, wv_ref, al_ref, dt_ref,
                       x_ref, qk_ref, kdt_ref, qg_ref, eg_ref, xs_ref, *, ts):
    i = pl.program_id(1)
    p = pl.program_id(2)
    base = GDN_HALO - (GDN_CONV_K - 1)

    def conv_silu(x_ref, h_ref, w_ref):
        xs_ref[0:GDN_HALO, :] = jnp.where(i > 0, h_ref[0], 0.0)
        xs_ref[GDN_HALO:GDN_HALO + ts, :] = x_ref[0]
        acc = jnp.zeros((ts, LANES), F32)
        for k in range(GDN_CONV_K):
            acc = acc + xs_ref[pl.ds(base + k, ts), :] * w_ref[k:k + 1, :]
        return _silu(acc)

    q, k, v = conv_silu(q_ref, hq_ref, wq_ref), conv_silu(k_ref, hk_ref, wk_ref), conv_silu(v_ref, hv_ref, wv_ref)

    r2 = _iota((2 * LANES, 2 * LANES), 0)
    c2 = _iota((2 * LANES, 2 * LANES), 1)
    head_sum = jnp.where(_div64(r2) == _div64(c2), 1.0, 0.0).astype(BF16)
    ssq = _dot_data_01(jnp.concatenate([q * q, k * k], axis=1), head_sum)
    qn = q * lax.rsqrt(ssq[:, :LANES] + EPS) * (HEAD_DIM ** -0.5)
    kn = k * lax.rsqrt(ssq[:, LANES:] + EPS)

    ab = ab_ref[0]
    lane = _iota(ab.shape, 1)
    nh = 6
    xa = ab + dt_ref[...]
    softplus = jnp.maximum(xa, 0.0) + jnp.log1p(jnp.exp(-jnp.abs(xa)))
    g_beta = jnp.where(lane < nh, -jnp.exp(al_ref[...]) * softplus,
                       jnp.where(lane < 2 * nh, jax.nn.sigmoid(ab), 0.0))
    pr = _iota((LANES, 2 * LANES), 0)
    pc = _iota((LANES, 2 * LANES), 1)
    pick = jnp.where(pr == 2 * p + _div64(pc & (LANES - 1)) + jnp.where(pc < LANES, 0, nh), 1.0, 0.0).astype(BF16)
    gbx = _dot_data_01(g_beta, pick)
    gx, bx = gbx[:, :LANES], gbx[:, LANES:]

    tr = _iota((ts, ts), 0)
    tc = _iota((ts, ts), 1)
    in_chunk_tril = jnp.where((_div64(tr) == _div64(tc)) & (tc <= tr), 1.0, 0.0).astype(BF16)
    gcx = _dot_01_data(in_chunk_tril, gx)
    nck = ts // GDN_CHUNK
    last = gcx.reshape(nck, GDN_CHUNK, LANES)[:, GDN_CHUNK - 1:GDN_CHUNK, :]
    glx = jnp.broadcast_to(last, (nck, GDN_CHUNK, LANES)).reshape(ts, LANES)
    eg = jnp.exp(gcx)

    kb = kn * bx
    qg_ref[0, 0] = (qn * eg).astype(BF16)
    eg_ref[0, 0] = jnp.exp(glx)
    _gdn_solve_tile(qn.astype(BF16), kn.astype(BF16), kb.astype(BF16), v * bx, kb * eg, kn * jnp.exp(glx - gcx), gcx,
                    x_ref, qk_ref, kdt_ref)


def _gdn_chunks(u, conv_w, a_log, dt_bias, *, q_lane_block, ab_lane_block, ts):
    b, s, _ = u.shape
    qb = q_lane_block
    hb = ts // GDN_HALO
    nh = a_log.shape[0]
    pad = lambda t: jnp.zeros((1, LANES), F32).at[0, :nh].set(t)
    cur = lambda off: pl.BlockSpec((1, ts, LANES), lambda bi, i, p: (bi, i, qb + off + p))
    halo = lambda off: pl.BlockSpec((1, GDN_HALO, LANES), lambda bi, i, p: (bi, jnp.maximum(i * hb - 1, 0), qb + off + p))
    wsp = lambda off: pl.BlockSpec((GDN_CONV_K, LANES), lambda bi, i, p: (0, off + p))
    vec = pl.BlockSpec((1, LANES), lambda bi, i, p: (0, 0))
    one = pl.BlockSpec((1, 1, ts, LANES), lambda bi, i, p: (bi, p, i, 0))
    two = pl.BlockSpec((1, 1, 2 * ts, LANES), lambda bi, i, p: (bi, p, i, 0))
    sh1 = lambda dt: jax.ShapeDtypeStruct((b, 3, s, LANES), dt)
    sh2 = lambda dt: jax.ShapeDtypeStruct((b, 3, 2 * s, LANES), dt)
    return pl.pallas_call(
        functools.partial(_gdn_chunks_kernel, ts=ts),
        grid=(b, s // ts, 3),
        in_specs=[cur(0), cur(3), cur(6), halo(0), halo(3), halo(6),
                  pl.BlockSpec((1, ts, LANES), lambda bi, i, p: (bi, i, ab_lane_block)),
                  wsp(0), wsp(3), wsp(6), vec, vec],
        out_specs=[two, two, two, one, one],
        out_shape=[sh2(F32), sh2(BF16), sh2(BF16), sh1(BF16), sh1(F32)],
        scratch_shapes=[pltpu.VMEM((GDN_HALO + ts, LANES), F32)],
        name="gdn_chunks",
        compiler_params=_cparams(("arbitrary", "arbitrary", "arbitrary")),
    )(u, u, u, u, u, u, u, conv_w, conv_w, conv_w, pad(a_log), pad(dt_bias))


def _stack2(top, bot):
    return jnp.concatenate([top, bot], axis=0)


def _gdn_solve_tile(qn_t, kn_t, kb_t, vb_t, kbg_t, kd_t, gc_t, x_ref, qk_ref, kdt_ref):
    cs = GDN_CHUNK
    n2 = 2 * cs
    chunks = qn_t.shape[0] // cs
    lane = _iota((cs, LANES), 1)
    low = lane < HEAD_DIM
    r = _iota((n2, n2), 0)
    c = _iota((n2, n2), 1)
    bd = _div64(r) == _div64(c)
    eye = jnp.where(r == c, 1.0, 0.0)
    diag8 = (r >> 3) == (c >> 3)
    merge_masks = [((r >> (sh + 1)) == (c >> (sh + 1))) & (((r >> sh) & 1) == 1) & (((c >> sh) & 1) == 0)
                   for sh in (3, 4, 5)]

    def per_head_rows(t, fill=0.0):
        return _stack2(jnp.where(low, t, fill), jnp.where(low, fill, t))

    cis = range(chunks)
    rows = [slice(ci * cs, (ci + 1) * cs) for ci in cis]
    rows2 = [pl.ds(ci * n2, n2) for ci in cis]
    kn = [per_head_rows(kn_t[rw]) for rw in rows]
    kb = [per_head_rows(kb_t[rw]) for rw in rows]
    qn = [per_head_rows(qn_t[rw]) for rw in rows]
    kk = [_dot_nt(kb[ci], kn[ci]) for ci in cis]
    qk = [_dot_nt(qn[ci], kn[ci]) for ci in cis]
    decay = []
    for rw in rows:
        gc = gc_t[rw]
        gsw = pltpu.roll(gc, HEAD_DIM, 1)
        gcol = _stack2(jnp.where(low, gc, gsw), jnp.where(low, gsw, gc))
        decay.append(jnp.exp(jnp.minimum(gcol - gcol.T, 0.0)))
    lmat = [jnp.where(c < r, kk[ci] * decay[ci], 0.0) for ci in cis]
    for ci in cis:
        qk_ref[0, 0, rows2[ci], :] = jnp.where(c <= r, qk[ci] * decay[ci], 0.0).astype(BF16)

    ld = [jnp.where(diag8, lm, 0.0) for lm in lmat]
    ld2 = [_dot_split2(m, m) for m in ld]
    ld4 = [_dot_split2(m, m) for m in ld2]
    t = [eye - m for m in ld]
    t = [t[ci] + _dot_split2(t[ci], ld2[ci]) for ci in cis]
    t = [t[ci] + _dot_split2(t[ci], ld4[ci]) for ci in cis]
    for lower_left in merge_masks:
        tb = [m.astype(BF16) for m in t]
        inner = [_dot(jnp.where(lower_left, lmat[ci], 0.0).astype(BF16), tb[ci]) for ci in cis]
        t = [t[ci] - _dot(tb[ci], inner[ci].astype(BF16)) for ci in cis]
    for ci in cis:
        kbg = kbg_t[rows[ci]]
        vsw = pltpu.roll(vb_t[rows[ci]], HEAD_DIM, 1)
        x = jnp.where(bd, _stack2(kbg, kbg), _stack2(vsw, vsw))
        x_ref[0, 0, rows2[ci], :] = _dot(t[ci].astype(BF16), x.astype(BF16))
    for ci in cis:
        kd = kd_t[rows[ci]]
        kdt_ref[0, 0, rows2[ci], :] = jnp.where(bd, _stack2(kd, kd), 0.0).T.astype(BF16)


def _gdn_scan_kernel(x_ref, qk_ref, kdt_ref, qg_ref, eg_ref, z0_ref, z1_ref, z2_ref, ng_ref, o_ref, st_ref,
                     *, chunks, batch):
    cs = GDN_CHUNK
    n2 = 2 * cs
    r = _iota((n2, n2), 0)
    c = _iota((n2, n2), 1)
    bd = _div64(r) == _div64(c)
    hr = _iota((LANES, LANES), 0)
    hc = _iota((LANES, LANES), 1)
    head_mean = jnp.where(_div64(hr) == _div64(hc), 1.0 / HEAD_DIM, 0.0).astype(BF16)
    z_refs = (z0_ref, z1_ref, z2_ref)

    @pl.when(pl.program_id(0) == 0)
    def _():
        st_ref[...] = jnp.zeros_like(st_ref)

    def body(ci, carry):
        start = pl.multiple_of(ci * cs, cs)
        rows = pl.ds(start, cs)
        rows2 = pl.ds(pl.multiple_of(ci * n2, n2), n2)
        chains = [(bi, p) for bi in range(batch) for p in range(3)]
        ks = range(len(chains))
        st = [st_ref[bi * 3 + p] for bi, p in chains]
        stb = [t.astype(BF16) for t in st]
        x = [x_ref[bi, p, rows2, :] for bi, p in chains]
        ws = [_dot(jnp.where(bd, x[k], 0.0).astype(BF16), stb[k]) for k in ks]
        vnb = [(jnp.where(bd, 0.0, x[k]) - ws[k]).astype(BF16) for k in ks]
        qbd = [jnp.where(bd, _stack2(qg, qg), jnp.zeros((), BF16)) for qg in (qg_ref[bi, p, rows, :] for bi, p in chains)]
        o = [_dot(qbd[k], stb[k]) + _dot(qk_ref[bi, p, rows2, :], vnb[k]) for k, (bi, p) in enumerate(chains)]
        upd = [_dot(kdt_ref[bi, p, rows2, :], vnb[k]) for k, (bi, p) in enumerate(chains)]
        for k, (bi, p) in enumerate(chains):
            eg = pltpu.roll(eg_ref[bi, p, pl.ds(start, 8), :], HEAD_DIM, 1)[0:1, :]
            st_ref[bi * 3 + p] = st[k] * eg + upd[k]
        op = [pltpu.roll(t[:cs] + t[cs:], HEAD_DIM, 1) for t in o]
        ms = [_dot_data_01(t * t, head_mean) for t in op]
        for k, (bi, p) in enumerate(chains):
            z = z_refs[p][bi, rows, :]
            o_ref[bi, p, rows, :] = op[k] * lax.rsqrt(ms[k] + EPS) * ng_ref[...] * _silu(z)
        return carry

    lax.fori_loop(0, chunks, body, 0)


def _gdn_scan(x, qk, kdt, qg, eg, u, norm_g, *, z_lane_block, ts):
    b, _, s, _ = qg.shape
    chunks = ts // GDN_CHUNK
    zb = z_lane_block
    two = pl.BlockSpec((b, 3, 2 * ts, LANES), lambda i: (0, 0, i, 0))
    one = pl.BlockSpec((b, 3, ts, LANES), lambda i: (0, 0, i, 0))
    zsp = lambda p: pl.BlockSpec((b, ts, LANES), lambda i: (0, i, zb + p))
    ng = jnp.concatenate([norm_g, norm_g]).reshape(1, LANES)
    return pl.pallas_call(
        functools.partial(_gdn_scan_kernel, chunks=chunks, batch=b),
        grid=(s // ts,),
        in_specs=[two, two, two, one, one, zsp(0), zsp(1), zsp(2), pl.BlockSpec((1, LANES), lambda i: (0, 0))],
        out_specs=one,
        out_shape=jax.ShapeDtypeStruct((b, 3, s, LANES), F32),
        scratch_shapes=[pltpu.VMEM((b * 3, LANES, LANES), F32)],
        name="gdn_scan",
        compiler_params=_cparams(("arbitrary",)),
    )(x, qk, kdt, qg, eg, u, u, u, ng)


def _outproj_kernel(x_ref, mod_ref, yc_ref, ym_ref, yg_ref, w_ref, o_ref):
    parts = [yc_ref[0]] + [y_ref[0, p] for y_ref in (ym_ref, yg_ref) for p in range(3)]
    y = jnp.concatenate([t.astype(BF16) for t in parts], axis=-1)
    o_ref[0] = x_ref[0] + mod_ref[0][5:6] * _dot(y, w_ref[...])


def _outproj(x, mod, yc, ym, yg, w, *, tm):
    b, s, d = x.shape
    ch = yc.shape[-1]
    return pl.pallas_call(
        _outproj_kernel,
        grid=(b, s // tm),
        in_specs=[pl.BlockSpec((1, tm, d), lambda bi, i: (bi, i, 0)),
                  pl.BlockSpec((1, N_MOD, d), lambda bi, i: (bi, 0, 0)),
                  pl.BlockSpec((1, tm, ch), lambda bi, i: (bi, i, 0)),
                  pl.BlockSpec((1, 3, tm, LANES), lambda bi, i: (bi, 0, i, 0)),
                  pl.BlockSpec((1, 3, tm, LANES), lambda bi, i: (bi, 0, i, 0)),
                  pl.BlockSpec(w.shape, lambda bi, i: (0, 0), pipeline_mode=pl.Buffered(1))],
        out_specs=pl.BlockSpec((1, tm, d), lambda bi, i: (bi, i, 0)),
        out_shape=jax.ShapeDtypeStruct((b, s, d), F32),
        name="out_proj",
        compiler_params=_cparams(("arbitrary", "arbitrary")),
    )(x, mod, yc, ym, yg, w)


def _token_mixing(x, mod, ln_g, w_in, conv_w, conv_b, conv_ln_g, conv_ln_b, gdn_conv_w, gdn_a_log, gdn_dt_bias,
                  gdn_norm_g, w_out, tables, *, tm, ts_conv, ts_moba, ts_gdn, ts_scan):
    d, d_in = w_in.shape
    conv_ch = conv_w.shape[1]
    n_pad = -(-d_in // LANES) * LANES
    w_in_p = jnp.zeros((d, n_pad), BF16).at[:, :d_in].set(w_in.astype(BF16))
    u = _inproj(x, mod, ln_g, w_in_p, tm=tm)

    moba_w = 3 * LANES
    q_blk_moba = 2 * conv_ch // LANES
    q_blk_gdn = q_blk_moba + 3 * moba_w // LANES
    z_blk = q_blk_gdn + 9
    ab_blk = z_blk + 3

    y_conv = _conformer_conv(u, conv_w, conv_b, conv_ln_g, conv_ln_b, ts=ts_conv)

    qp, ka, va, km = _moba_prep(u, tables, q_lane_block=q_blk_moba, ts=ts_moba)
    y_moba = _moba_attention(qp, ka, va, _block_mean_rows(km))

    xs, qk, kdt, qg, eg = _gdn_chunks(u, gdn_conv_w, gdn_a_log, gdn_dt_bias, q_lane_block=q_blk_gdn,
                                       ab_lane_block=ab_blk, ts=ts_gdn)
    y_gdn = _gdn_scan(xs, qk, kdt, qg, eg, u, gdn_norm_g, z_lane_block=z_blk, ts=ts_scan)

    return _outproj(x, mod, y_conv, y_moba, y_gdn, w_out.astype(BF16), tm=tm)


def kernel(x, c, w_ada, b_ada, ln_ffn1_g, ffn1_w_gate, ffn1_w_up, ffn1_w_down, ln_mix_g, w_in, conv_w, conv_b, conv_ln_g, conv_ln_b, gdn_conv_w, gdn_a_log, gdn_dt_bias, gdn_norm_g, w_out, ln_ffn2_g, ffn2_w_gate, ffn2_w_up, ffn2_w_down, final_g):
    depth = w_ada.shape[0]
    s = x.shape[1]
    tm = min(512, s)
    tiles = dict(tm=tm, ts_conv=min(512, s), ts_moba=min(1024, s), ts_gdn=min(512, s), ts_scan=min(512, s))
    mods = _ada_mod(c, w_ada, b_ada)
    tables = _rope_tables(s)
    for l in range(depth):
        mod = mods[l]
        x = _ffn(x, mod, ln_ffn1_g, ffn1_w_gate, ffn1_w_up, ffn1_w_down, final_g,
                 layer=l, mod_base=0, final=False, tm=tm)
        x = _token_mixing(x, mod, ln_mix_g[l], w_in[l], conv_w[l], conv_b[l], conv_ln_g[l], conv_ln_b[l],
                          gdn_conv_w[l], gdn_a_log[l], gdn_dt_bias[l], gdn_norm_g[l], w_out[l], tables, **tiles)
        x = _ffn(x, mod, ln_ffn2_g, ffn2_w_gate, ffn2_w_up, ffn2_w_down, final_g,
                 layer=l, mod_base=6, final=(l == depth - 1), tm=tm)
    return x
```

```python
import functools
import math

import numpy as np
import jax
import jax.numpy as jnp
from jax import lax
from jax.experimental import pallas as pl
from jax.experimental.pallas import tpu as pltpu

F32 = jnp.float32
BF16 = jnp.bfloat16

EPS = 1e-6
HEAD_DIM = 64
LANES = 128
SUBLANES = 8
N_MOD = 9
FFN_CHUNK = 256
CONV_K = 31
CONV_HALO = 32
MOBA_BLOCK = 256
MOBA_TOPK = 3
MOBA_GROUP = 4
ROPE_DIM = HEAD_DIM // 4
ROPE_THETA = 500000.0
GDN_CONV_K = 4
GDN_HALO = 8
GDN_CHUNK = 64
MASK_NEG = -(2.0 ** 100)
VMEM_LIMIT = 56 * 1024 * 1024


def _cparams(sem):
    return pltpu.CompilerParams(dimension_semantics=sem, vmem_limit_bytes=VMEM_LIMIT)


def _iota(shape, dim):
    return lax.broadcasted_iota(jnp.int32, shape, dim)


def _div64(t):
    return t >> 6


def _dot(a, b):
    return jnp.dot(a, b, preferred_element_type=F32)


def _dot_nt(a, b):
    return lax.dot_general(a, b, (((1,), (1,)), ((), ())), preferred_element_type=F32)


def _split3(x):
    x0 = x.astype(BF16)
    r1 = x - x0.astype(F32)
    x1 = r1.astype(BF16)
    x2 = (r1 - x1.astype(F32)).astype(BF16)
    return x0, x1, x2


def _dot_nt_split3(a, b):
    a0, a1, a2 = _split3(a)
    b0, b1, b2 = _split3(b)
    small = _dot_nt(a0, b2) + _dot_nt(a1, b1) + _dot_nt(a2, b0)
    mid = _dot_nt(a0, b1) + _dot_nt(a1, b0)
    return _dot_nt(a0, b0) + (mid + small)


def _dot_data_01(x, m01):
    x0, x1, x2 = _split3(x)
    return _dot(x0, m01) + (_dot(x1, m01) + _dot(x2, m01))


def _dot_01_data(m01, x):
    x0, x1, x2 = _split3(x)
    return _dot(m01, x0) + (_dot(m01, x1) + _dot(m01, x2))


def _dot_split2(a, b):
    a0 = a.astype(BF16)
    a1 = (a - a0.astype(F32)).astype(BF16)
    b0 = b.astype(BF16)
    b1 = (b - b0.astype(F32)).astype(BF16)
    return _dot(a0, b0) + (_dot(a0, b1) + _dot(a1, b0))


def _silu(x):
    return x * jax.nn.sigmoid(x)


def _modulate(x, g, shift, scale):
    ms = jnp.mean(x * x, axis=-1, keepdims=True)
    y = x * lax.rsqrt(ms + EPS) * g
    return y * (1.0 + scale) + shift


def _ada_kernel(c_ref, w_ref, b_ref, o_ref):
    ca = _silu(c_ref[...]).astype(BF16)
    o_ref[0] = _dot(ca, w_ref[0].astype(BF16)) + b_ref[0]


def _ada_mod(c, w_ada, b_ada):
    depth, d, nd = w_ada.shape
    b = c.shape[0]
    rows = 8
    c8 = jnp.zeros((rows, d), F32).at[:b].set(c)
    out = pl.pallas_call(
        _ada_kernel,
        grid=(depth, nd // d),
        in_specs=[pl.BlockSpec((rows, d), lambda l, j: (0, 0)),
                  pl.BlockSpec((1, d, d), lambda l, j: (l, 0, j)),
                  pl.BlockSpec((1, 1, d), lambda l, j: (l, 0, j))],
        out_specs=pl.BlockSpec((1, rows, d), lambda l, j: (l, 0, j)),
        out_shape=jax.ShapeDtypeStruct((depth, rows, nd), F32),
        name="ada_mod",
        compiler_params=_cparams(("arbitrary", "arbitrary")),
    )(c8, w_ada, b_ada.reshape(depth, 1, nd))
    return out[:, :b].reshape(depth, b, nd // d, d)


def _ffn_kernel(x_ref, mod_ref, g_ref, wg_ref, wu_ref, wd_ref, fg_ref, o_ref, *, mod_base, final):
    x = x_ref[0]
    m = mod_ref[0]
    shift, scale, gate = m[mod_base:mod_base + 1], m[mod_base + 1:mod_base + 2], m[mod_base + 2:mod_base + 3]
    h = _modulate(x, g_ref[0], shift, scale).astype(BF16)
    down = None
    for c0 in range(0, wg_ref.shape[2], FFN_CHUNK):
        a = _dot(h, wg_ref[0, :, c0:c0 + FFN_CHUNK].astype(BF16))
        u = _dot(h, wu_ref[0, :, c0:c0 + FFN_CHUNK].astype(BF16))
        act = (_silu(a) * u).astype(BF16)
        part = _dot(act, wd_ref[0, c0:c0 + FFN_CHUNK, :].astype(BF16))
        down = part if down is None else down + part
    y = x + (0.5 * gate) * down
    if final:
        y = y * lax.rsqrt(jnp.mean(y * y, axis=-1, keepdims=True) + EPS) * fg_ref[...]
    o_ref[0] = y


def _ffn(x, mod, g, wg, wu, wd, final_g, *, layer, mod_base, final, tm):
    b, s, d = x.shape
    depth, _, f = wg.shape
    resident = dict(pipeline_mode=pl.Buffered(1))
    return pl.pallas_call(
        functools.partial(_ffn_kernel, mod_base=mod_base, final=final),
        grid=(b, s // tm),
        in_specs=[pl.BlockSpec((1, tm, d), lambda bi, i: (bi, i, 0)),
                  pl.BlockSpec((1, N_MOD, d), lambda bi, i: (bi, 0, 0)),
                  pl.BlockSpec((1, 1, d), lambda bi, i: (layer, 0, 0)),
                  pl.BlockSpec((1, d, f), lambda bi, i: (layer, 0, 0), **resident),
                  pl.BlockSpec((1, d, f), lambda bi, i: (layer, 0, 0), **resident),
                  pl.BlockSpec((1, f, d), lambda bi, i: (layer, 0, 0), **resident),
                  pl.BlockSpec((1, d), lambda bi, i: (0, 0))],
        out_specs=pl.BlockSpec((1, tm, d), lambda bi, i: (bi, i, 0)),
        out_shape=jax.ShapeDtypeStruct((b, s, d), F32),
        name="ffn",
        compiler_params=_cparams(("arbitrary", "arbitrary")),
    )(x, mod, g.reshape(depth, 1, d), wg, wu, wd, final_g.reshape(1, d))


def _inproj_kernel(x_ref, mod_ref, g_ref, w_ref, o_ref):
    m = mod_ref[0]
    h = _modulate(x_ref[0], g_ref[...], m[3:4], m[4:5]).astype(BF16)
    o_ref[0] = _dot(h, w_ref[...])


def _inproj(x, mod, g, w, *, tm):
    b, s, d = x.shape
    n = w.shape[1]
    return pl.pallas_call(
        _inproj_kernel,
        grid=(b, s // tm),
        in_specs=[pl.BlockSpec((1, tm, d), lambda bi, i: (bi, i, 0)),
                  pl.BlockSpec((1, N_MOD, d), lambda bi, i: (bi, 0, 0)),
                  pl.BlockSpec((1, d), lambda bi, i: (0, 0)),
                  pl.BlockSpec((d, n), lambda bi, i: (0, 0), pipeline_mode=pl.Buffered(1))],
        out_specs=pl.BlockSpec((1, tm, n), lambda bi, i: (bi, i, 0)),
        out_shape=jax.ShapeDtypeStruct((b, s, n), F32),
        name="in_proj",
        compiler_params=_cparams(("arbitrary", "arbitrary")),
    )(x, mod, g.reshape(1, d), w)


def _conv_kernel(u_ref, p_ref, w_ref, b_ref, lg_ref, lb_ref, o_ref, hs_ref, sh_ref, *, ts, ch):
    i = pl.program_id(1)
    cur = u_ref[0]
    prev = p_ref[0]
    hprev = prev[:, :ch] * jax.nn.sigmoid(prev[:, ch:])
    hs_ref[0:CONV_HALO, :] = jnp.where(i > 0, hprev, 0.0)
    hs_ref[CONV_HALO:CONV_HALO + ts, :] = cur[:, :ch] * jax.nn.sigmoid(cur[:, ch:])
    hs_ref[CONV_HALO + ts:CONV_HALO + ts + SUBLANES, :] = jnp.zeros((SUBLANES, ch), F32)
    span = CONV_HALO + ts
    for r in range(SUBLANES):
        sh_ref[r] = hs_ref[pl.ds(r, span), :]
    base = CONV_HALO - (CONV_K - 1)
    acc = jnp.zeros((ts, ch), F32) + b_ref[...]
    for k in range(CONV_K):
        r = (base + k) % SUBLANES
        acc = acc + sh_ref[r, pl.ds(base + k - r, ts), :] * w_ref[k:k + 1, :]
    mu = jnp.mean(acc, axis=-1, keepdims=True)
    cen = acc - mu
    var = jnp.mean(cen * cen, axis=-1, keepdims=True)
    o_ref[0] = _silu(cen * lax.rsqrt(var + EPS) * lg_ref[...] + lb_ref[...])


def _conformer_conv(u, w, bias, ln_g, ln_b, *, ts):
    b, s, _ = u.shape
    ch = w.shape[1]
    hb = ts // CONV_HALO
    return pl.pallas_call(
        functools.partial(_conv_kernel, ts=ts, ch=ch),
        grid=(b, s // ts),
        in_specs=[pl.BlockSpec((1, ts, 2 * ch), lambda bi, i: (bi, i, 0)),
                  pl.BlockSpec((1, CONV_HALO, 2 * ch), lambda bi, i: (bi, jnp.maximum(i * hb - 1, 0), 0)),
                  pl.BlockSpec((CONV_K, ch), lambda bi, i: (0, 0)),
                  pl.BlockSpec((1, ch), lambda bi, i: (0, 0)),
                  pl.BlockSpec((1, ch), lambda bi, i: (0, 0)),
                  pl.BlockSpec((1, ch), lambda bi, i: (0, 0))],
        out_specs=pl.BlockSpec((1, ts, ch), lambda bi, i: (bi, i, 0)),
        out_shape=jax.ShapeDtypeStruct((b, s, ch), F32),
        scratch_shapes=[pltpu.VMEM((CONV_HALO + ts + SUBLANES, ch), F32),
                        pltpu.VMEM((SUBLANES, CONV_HALO + ts, ch), F32)],
        name="conformer_conv",
        compiler_params=_cparams(("arbitrary", "arbitrary")),
    )(u, u, w, bias.reshape(1, ch), ln_g.reshape(1, ch), ln_b.reshape(1, ch))


def _rope_tables(s):
    half = ROPE_DIM // 2
    inv = jnp.exp(-math.log(ROPE_THETA) * jnp.arange(0, ROPE_DIM, 2, dtype=F32) / ROPE_DIM)
    ang = jnp.arange(s, dtype=F32)[:, None] * inv[None, :]
    cs = jnp.concatenate([jnp.cos(ang), jnp.sin(ang), jnp.ones((s, 1), F32)], axis=-1)
    d = np.arange(LANES) % HEAD_DIM
    rows = 2 * half + 1
    mc, msa, msb = (np.zeros((rows, LANES), np.float32) for _ in range(3))
    for lane_i, di in enumerate(d):
        if di < ROPE_DIM:
            mc[di % half, lane_i] = 1.0
        else:
            mc[2 * half, lane_i] = 1.0
        if di < half:
            msa[half + di, lane_i] = -1.0
        elif di < ROPE_DIM:
            msb[half + (di - half), lane_i] = 1.0
    spread = lambda m: jnp.dot(cs, jnp.asarray(m), precision=lax.Precision.HIGHEST)
    return spread(mc), spread(msa), spread(msb)


def _mask_base(hh):
    return (1 - hh) * HEAD_DIM


def _moba_prep_kernel(q_ref, k_ref, v_ref, c_ref, sa_ref, sb_ref, qo_ref, ko_ref, vo_ref, km_ref):
    i = pl.program_id(0)
    half = ROPE_DIM // 2
    c, sa, sb = c_ref[...], sa_ref[...], sb_ref[...]

    def rope(x):
        return x * c + pltpu.roll(x, LANES - half, 1) * sa + pltpu.roll(x, half, 1) * sb

    qr, kr, v = rope(q_ref[0]), rope(k_ref[0]), v_ref[0]
    lane = _iota(qr.shape, 1)
    nbs = qr.shape[0] // MOBA_BLOCK
    blk_of_row = i * nbs + (_iota(qr.shape, 0) >> (MOBA_BLOCK.bit_length() - 1))
    for hh in range(2):
        own = (lane >= hh * HEAD_DIM) & (lane < (hh + 1) * HEAD_DIM)
        mb = _mask_base(hh)
        kz = jnp.where(own, kr, 0.0)
        qo_ref[0, hh] = jnp.where(own, qr, 0.0)
        ko_ref[0, hh] = jnp.where(own, kr, jnp.where(lane == mb + blk_of_row, 1.0, 0.0)).astype(BF16)
        vo_ref[0, hh] = jnp.where(own, v, jnp.where(lane == mb, 1.0, 0.0)).astype(BF16)
        for j in range(nbs):
            km_ref[0, hh, j] = jnp.mean(kz[j * MOBA_BLOCK:(j + 1) * MOBA_BLOCK], axis=0, keepdims=True)


def _moba_prep(u, tables, *, q_lane_block, ts):
    b, s, _ = u.shape
    nb = s // MOBA_BLOCK
    nbs = ts // MOBA_BLOCK
    c, sa, sb = tables
    qb = q_lane_block
    tab = pl.BlockSpec((ts, LANES), lambda i, bi, p: (i, 0))
    head = lambda dt: jax.ShapeDtypeStruct((b, 6, s, LANES), dt)
    return pl.pallas_call(
        _moba_prep_kernel,
        grid=(s // ts, b, 3),
        in_specs=[pl.BlockSpec((1, ts, LANES), lambda i, bi, p: (bi, i, qb + p)),
                  pl.BlockSpec((1, ts, LANES), lambda i, bi, p: (bi, i, qb + 3 + p)),
                  pl.BlockSpec((1, ts, LANES), lambda i, bi, p: (bi, i, qb + 6 + p)),
                  tab, tab, tab],
        out_specs=[pl.BlockSpec((1, 2, ts, LANES), lambda i, bi, p: (bi, p, i, 0)),
                   pl.BlockSpec((1, 2, ts, LANES), lambda i, bi, p: (bi, p, i, 0)),
                   pl.BlockSpec((1, 2, ts, LANES), lambda i, bi, p: (bi, p, i, 0)),
                   pl.BlockSpec((1, 2, nbs, 1, LANES), lambda i, bi, p: (bi, p, i, 0, 0))],
        out_shape=[head(F32), head(BF16), head(BF16), jax.ShapeDtypeStruct((b, 6, nb, 1, LANES), F32)],
        name="moba_prep",
        compiler_params=_cparams(("arbitrary", "arbitrary", "arbitrary")),
    )(u, u, u, c, sa, sb)


def _block_mean_rows(km):
    b, h, nb = km.shape[:3]
    km = km[:, :, :, 0, :].reshape(b, h // 2, 2, nb, LANES)
    rows = [jnp.zeros((b, h // 2, LANES, LANES), F32).at[:, :, _mask_base(hh):_mask_base(hh) + nb, :].set(km[:, :, hh])
            for hh in range(2)]
    return jnp.stack(rows, axis=2).reshape(b, h, LANES, LANES)


def _moba_kernel(q_ref, ka_ref, va_ref, kmt_ref, o_ref, *, tq):
    i = pl.program_id(2)
    blk_shift = MOBA_BLOCK.bit_length() - 1
    gk = MOBA_GROUP * MOBA_BLOCK
    first_blk = i * (tq // MOBA_BLOCK)
    lane = _iota((tq, LANES), 1)
    nsel = LANES // 4
    blk_id = _iota((nsel, tq), 0)
    blk_f = blk_id.astype(F32)
    own_blk = first_blk + (_iota((nsel, tq), 1) >> blk_shift)
    neg_inf = -jnp.inf

    qas = []
    for hh in range(2):
        mb = _mask_base(hh)
        qp = q_ref[0, hh]
        gate = _dot_nt_split3(kmt_ref[0, hh], qp)[mb:mb + nsel, :]
        s = jnp.where(blk_id < own_blk, gate, neg_inf)
        sel = jnp.where(blk_id == own_blk, 1.0, 0.0)
        for _ in range(MOBA_TOPK):
            m = jnp.max(s, axis=0, keepdims=True)
            idx = jnp.min(jnp.where(s == m, blk_f, float(LANES)), axis=0, keepdims=True)
            hit = blk_f == idx
            sel = jnp.where(hit, 1.0, sel)
            s = jnp.where(hit, neg_inf, s)
        neg_t = jnp.where(sel > 0.5, 0.0, MASK_NEG)
        pad_lo = jnp.zeros((mb, tq), F32)
        pad_hi = jnp.zeros((LANES - mb - nsel, tq), F32)
        negm = jnp.concatenate([t for t in (pad_lo, neg_t, pad_hi) if t.shape[0]], axis=0).T
        qas.append((qp * (HEAD_DIM ** -0.5 * math.log2(math.e)) + negm).astype(BF16))

    def step(carry, start, causal):
        hs = range(2)
        s = [_dot_nt(qas[hh], ka_ref[0, hh, pl.ds(start, gk), :]) for hh in hs]
        if causal:
            keep = start + _iota((tq, gk), 1) <= i * tq + _iota((tq, gk), 0)
            s = [jnp.where(keep, t, MASK_NEG) for t in s]
        mn = [jnp.maximum(carry[2 * hh], jnp.max(s[hh], axis=-1, keepdims=True)) for hh in hs]
        p = [jnp.exp2(s[hh] - mn[hh]).astype(BF16) for hh in hs]
        pv = [_dot(p[hh], va_ref[0, hh, pl.ds(start, gk), :]) for hh in hs]
        out = []
        for hh in hs:
            out += [mn[hh], jnp.exp2(carry[2 * hh] - mn[hh]) * carry[2 * hh + 1] + pv[hh]]
        return tuple(out)

    m_init = jnp.full((tq, 1), neg_inf, F32)
    acc_init = jnp.zeros((tq, LANES), F32)
    n_past = first_blk >> (MOBA_GROUP.bit_length() - 1)
    res = lax.fori_loop(0, n_past, lambda g, c: step(c, pl.multiple_of(g * gk, gk), False),
                        (m_init, acc_init, m_init, acc_init))
    res = step(res, pl.multiple_of(n_past * gk, gk), True)
    l0, l1 = _mask_base(0), _mask_base(1)
    o0 = res[1] / res[1][:, l0:l0 + 1]
    o1 = res[3] / res[3][:, l1:l1 + 1]
    o_ref[0, 0] = jnp.where(lane < HEAD_DIM, o0, o1)


def _moba_attention(qp, ka, va, kmt, *, tq):
    b, _, s, _ = qp.shape
    assert s % (MOBA_GROUP * MOBA_BLOCK) == 0 and (MOBA_GROUP * MOBA_BLOCK) % tq == 0 and tq % MOBA_BLOCK == 0
    assert s // MOBA_BLOCK <= LANES // 4
    blk = tq
    return pl.pallas_call(
        functools.partial(_moba_kernel, tq=tq),
        grid=(b, 3, s // tq),
        in_specs=[pl.BlockSpec((1, 2, blk, LANES), lambda bi, p, i: (bi, p, i, 0)),
                  pl.BlockSpec((1, 2, s, LANES), lambda bi, p, i: (bi, p, 0, 0)),
                  pl.BlockSpec((1, 2, s, LANES), lambda bi, p, i: (bi, p, 0, 0)),
                  pl.BlockSpec((1, 2, LANES, LANES), lambda bi, p, i: (bi, p, 0, 0))],
        out_specs=pl.BlockSpec((1, 1, blk, LANES), lambda bi, p, i: (bi, p, i, 0)),
        out_shape=jax.ShapeDtypeStruct((b, 3, s, LANES), F32),
        name="moba_attention",
        compiler_params=_cparams(("arbitrary", "arbitrary", "arbitrary")),
    )(qp, ka, va, kmt)


def _gdn_chunks_kernel(q_ref, k_ref, v_ref, hq_ref, hk_ref, hv_ref, ab_ref, wq_ref, wk_ref, wv_ref, al_ref, dt_ref,
                       x_ref, qk_ref, kdt_ref, qg_ref, eg_ref, xs_ref, *, ts):
    i = pl.program_id(1)
    p = pl.program_id(2)
    base = GDN_HALO - (GDN_CONV_K - 1)

    def conv_silu(x_ref, h_ref, w_ref):
        xs_ref[0:GDN_HALO, :] = jnp.where(i > 0, h_ref[0], 0.0)
        xs_ref[GDN_HALO:GDN_HALO + ts, :] = x_ref[0]
        acc = jnp.zeros((ts, LANES), F32)
        for k in range(GDN_CONV_K):
            acc = acc + xs_ref[pl.ds(base + k, ts), :] * w_ref[k:k + 1, :]
        return _silu(acc)

    q, k, v = conv_silu(q_ref, hq_ref, wq_ref), conv_silu(k_ref, hk_ref, wk_ref), conv_silu(v_ref, hv_ref, wv_ref)

    r2 = _iota((2 * LANES, 2 * LANES), 0)
    c2 = _iota((2 * LANES, 2 * LANES), 1)
    head_sum = jnp.where(_div64(r2) == _div64(c2), 1.0, 0.0).astype(BF16)
    ssq = _dot_data_01(jnp.concatenate([q * q, k * k], axis=1), head_sum)
    qn = q * lax.rsqrt(ssq[:, :LANES] + EPS) * (HEAD_DIM ** -0.5)
    kn = k * lax.rsqrt(ssq[:, LANES:] + EPS)

    ab = ab_ref[0]
    lane = _iota(ab.shape, 1)
    nh = 6
    xa = ab + dt_ref[...]
    softplus = jnp.maximum(xa, 0.0) + jnp.log1p(jnp.exp(-jnp.abs(xa)))
    g_beta = jnp.where(lane < nh, -jnp.exp(al_ref[...]) * softplus,
                       jnp.where(lane < 2 * nh, jax.nn.sigmoid(ab), 0.0))
    pr = _iota((LANES, 2 * LANES), 0)
    pc = _iota((LANES, 2 * LANES), 1)
    pick = jnp.where(pr == 2 * p + _div64(pc & (LANES - 1)) + jnp.where(pc < LANES, 0, nh), 1.0, 0.0).astype(BF16)
    gbx = _dot_data_01(g_beta, pick)
    gx, bx = gbx[:, :LANES], gbx[:, LANES:]

    sub = min(ts, 2 * LANES)
    tr = _iota((sub, sub), 0)
    tc = _iota((sub, sub), 1)
    in_chunk_tril = jnp.where((_div64(tr) == _div64(tc)) & (tc <= tr), 1.0, 0.0).astype(BF16)
    gcx = jnp.concatenate([_dot_01_data(in_chunk_tril, gx[r0:r0 + sub]) for r0 in range(0, ts, sub)],
                          axis=0)
    nck = ts // GDN_CHUNK
    last = gcx.reshape(nck, GDN_CHUNK, LANES)[:, GDN_CHUNK - 1:GDN_CHUNK, :]
    glx = jnp.broadcast_to(last, (nck, GDN_CHUNK, LANES)).reshape(ts, LANES)
    eg = jnp.exp(gcx)

    kb = kn * bx
    qg_ref[0, 0] = (qn * eg).astype(BF16)
    eg_ref[0, 0] = jnp.exp(glx)
    _gdn_solve_tile(qn.astype(BF16), kn.astype(BF16), kb.astype(BF16), v * bx, kb * eg, kn * jnp.exp(glx - gcx), gcx,
                    x_ref, qk_ref, kdt_ref)


def _gdn_chunks(u, conv_w, a_log, dt_bias, *, q_lane_block, ab_lane_block, ts):
    b, s, _ = u.shape
    qb = q_lane_block
    hb = ts // GDN_HALO
    nh = a_log.shape[0]
    pad = lambda t: jnp.zeros((1, LANES), F32).at[0, :nh].set(t)
    cur = lambda off: pl.BlockSpec((1, ts, LANES), lambda bi, i, p: (bi, i, qb + off + p))
    halo = lambda off: pl.BlockSpec((1, GDN_HALO, LANES), lambda bi, i, p: (bi, jnp.maximum(i * hb - 1, 0), qb + off + p))
    wsp = lambda off: pl.BlockSpec((GDN_CONV_K, LANES), lambda bi, i, p: (0, off + p))
    vec = pl.BlockSpec((1, LANES), lambda bi, i, p: (0, 0))
    one = pl.BlockSpec((1, 1, ts, LANES), lambda bi, i, p: (bi, p, i, 0))
    two = pl.BlockSpec((1, 1, 2 * ts, LANES), lambda bi, i, p: (bi, p, i, 0))
    sh1 = lambda dt: jax.ShapeDtypeStruct((b, 3, s, LANES), dt)
    sh2 = lambda dt: jax.ShapeDtypeStruct((b, 3, 2 * s, LANES), dt)
    return pl.pallas_call(
        functools.partial(_gdn_chunks_kernel, ts=ts),
        grid=(b, s // ts, 3),
        in_specs=[cur(0), cur(3), cur(6), halo(0), halo(3), halo(6),
                  pl.BlockSpec((1, ts, LANES), lambda bi, i, p: (bi, i, ab_lane_block)),
                  wsp(0), wsp(3), wsp(6), vec, vec],
        out_specs=[two, two, two, one, one],
        out_shape=[sh2(F32), sh2(BF16), sh2(BF16), sh1(BF16), sh1(F32)],
        scratch_shapes=[pltpu.VMEM((GDN_HALO + ts, LANES), F32)],
        name="gdn_chunks",
        compiler_params=_cparams(("arbitrary", "arbitrary", "arbitrary")),
    )(u, u, u, u, u, u, u, conv_w, conv_w, conv_w, pad(a_log), pad(dt_bias))


def _stack2(top, bot):
    return jnp.concatenate([top, bot], axis=0)


def _gdn_solve_tile(qn_t, kn_t, kb_t, vb_t, kbg_t, kd_t, gc_t, x_ref, qk_ref, kdt_ref):
    cs = GDN_CHUNK
    n2 = 2 * cs
    chunks = qn_t.shape[0] // cs
    lane = _iota((cs, LANES), 1)
    low = lane < HEAD_DIM
    r = _iota((n2, n2), 0)
    c = _iota((n2, n2), 1)
    bd = _div64(r) == _div64(c)
    eye = jnp.where(r == c, 1.0, 0.0)
    diag8 = (r >> 3) == (c >> 3)
    merge_masks = [((r >> (sh + 1)) == (c >> (sh + 1))) & (((r >> sh) & 1) == 1) & (((c >> sh) & 1) == 0)
                   for sh in (3, 4, 5)]

    def per_head_rows(t, fill=0.0):
        return _stack2(jnp.where(low, t, fill), jnp.where(low, fill, t))

    cis = range(chunks)
    rows = [slice(ci * cs, (ci + 1) * cs) for ci in cis]
    rows2 = [pl.ds(ci * n2, n2) for ci in cis]
    kn = [per_head_rows(kn_t[rw]) for rw in rows]
    kb = [per_head_rows(kb_t[rw]) for rw in rows]
    qn = [per_head_rows(qn_t[rw]) for rw in rows]
    kk = [_dot_nt(kb[ci], kn[ci]) for ci in cis]
    qk = [_dot_nt(qn[ci], kn[ci]) for ci in cis]
    decay = []
    for rw in rows:
        gc = gc_t[rw]
        gsw = pltpu.roll(gc, HEAD_DIM, 1)
        gcol = _stack2(jnp.where(low, gc, gsw), jnp.where(low, gsw, gc))
        decay.append(jnp.exp(jnp.minimum(gcol - gcol.T, 0.0)))
    lmat = [jnp.where(c < r, kk[ci] * decay[ci], 0.0) for ci in cis]
    for ci in cis:
        qk_ref[0, 0, rows2[ci], :] = jnp.where(c <= r, qk[ci] * decay[ci], 0.0).astype(BF16)

    ld = [jnp.where(diag8, lm, 0.0) for lm in lmat]
    ld2 = [_dot_split2(m, m) for m in ld]
    ld4 = [_dot_split2(m, m) for m in ld2]
    t = [eye - m for m in ld]
    t = [t[ci] + _dot_split2(t[ci], ld2[ci]) for ci in cis]
    t = [t[ci] + _dot_split2(t[ci], ld4[ci]) for ci in cis]
    for lower_left in merge_masks:
        tb = [m.astype(BF16) for m in t]
        inner = [_dot(jnp.where(lower_left, lmat[ci], 0.0).astype(BF16), tb[ci]) for ci in cis]
        t = [t[ci] - _dot(tb[ci], inner[ci].astype(BF16)) for ci in cis]
    for ci in cis:
        kbg = kbg_t[rows[ci]]
        vsw = pltpu.roll(vb_t[rows[ci]], HEAD_DIM, 1)
        x = jnp.where(bd, _stack2(kbg, kbg), _stack2(vsw, vsw))
        x_ref[0, 0, rows2[ci], :] = _dot(t[ci].astype(BF16), x.astype(BF16))
    for ci in cis:
        kd = kd_t[rows[ci]]
        kdt_ref[0, 0, rows2[ci], :] = jnp.where(bd, _stack2(kd, kd), 0.0).T.astype(BF16)


def _gdn_scan_kernel(x_ref, qk_ref, kdt_ref, qg_ref, eg_ref, z0_ref, z1_ref, z2_ref, ng_ref, o_ref, st_ref,
                     *, chunks, batch):
    cs = GDN_CHUNK
    n2 = 2 * cs
    r = _iota((n2, n2), 0)
    c = _iota((n2, n2), 1)
    bd = _div64(r) == _div64(c)
    hr = _iota((LANES, LANES), 0)
    hc = _iota((LANES, LANES), 1)
    head_mean = jnp.where(_div64(hr) == _div64(hc), 1.0 / HEAD_DIM, 0.0).astype(BF16)
    z_refs = (z0_ref, z1_ref, z2_ref)

    @pl.when(pl.program_id(0) == 0)
    def _():
        st_ref[...] = jnp.zeros_like(st_ref)

    def body(ci, carry):
        start = pl.multiple_of(ci * cs, cs)
        rows = pl.ds(start, cs)
        rows2 = pl.ds(pl.multiple_of(ci * n2, n2), n2)
        chains = [(bi, p) for bi in range(batch) for p in range(3)]
        ks = range(len(chains))
        st = [st_ref[bi * 3 + p] for bi, p in chains]
        stb = [t.astype(BF16) for t in st]
        x = [x_ref[bi, p, rows2, :] for bi, p in chains]
        ws = [_dot(jnp.where(bd, x[k], 0.0).astype(BF16), stb[k]) for k in ks]
        vnb = [(jnp.where(bd, 0.0, x[k]) - ws[k]).astype(BF16) for k in ks]
        qbd = [jnp.where(bd, _stack2(qg, qg), jnp.zeros((), BF16)) for qg in (qg_ref[bi, p, rows, :] for bi, p in chains)]
        o = [_dot(qbd[k], stb[k]) + _dot(qk_ref[bi, p, rows2, :], vnb[k]) for k, (bi, p) in enumerate(chains)]
        upd = [_dot(kdt_ref[bi, p, rows2, :], vnb[k]) for k, (bi, p) in enumerate(chains)]
        for k, (bi, p) in enumerate(chains):
            eg = pltpu.roll(eg_ref[bi, p, pl.ds(start, 8), :], HEAD_DIM, 1)[0:1, :]
            st_ref[bi * 3 + p] = st[k] * eg + upd[k]
        op = [pltpu.roll(t[:cs] + t[cs:], HEAD_DIM, 1) for t in o]
        ms = [_dot_data_01(t * t, head_mean) for t in op]
        for k, (bi, p) in enumerate(chains):
            z = z_refs[p][bi, rows, :]
            o_ref[bi, p, rows, :] = op[k] * lax.rsqrt(ms[k] + EPS) * ng_ref[...] * _silu(z)
        return carry

    lax.fori_loop(0, chunks, body, 0)


def _gdn_scan(x, qk, kdt, qg, eg, u, norm_g, *, z_lane_block, ts):
    b, _, s, _ = qg.shape
    chunks = ts // GDN_CHUNK
    zb = z_lane_block
    two = pl.BlockSpec((b, 3, 2 * ts, LANES), lambda i: (0, 0, i, 0))
    one = pl.BlockSpec((b, 3, ts, LANES), lambda i: (0, 0, i, 0))
    zsp = lambda p: pl.BlockSpec((b, ts, LANES), lambda i: (0, i, zb + p))
    ng = jnp.concatenate([norm_g, norm_g]).reshape(1, LANES)
    return pl.pallas_call(
        functools.partial(_gdn_scan_kernel, chunks=chunks, batch=b),
        grid=(s // ts,),
        in_specs=[two, two, two, one, one, zsp(0), zsp(1), zsp(2), pl.BlockSpec((1, LANES), lambda i: (0, 0))],
        out_specs=one,
        out_shape=jax.ShapeDtypeStruct((b, 3, s, LANES), F32),
        scratch_shapes=[pltpu.VMEM((b * 3, LANES, LANES), F32)],
        name="gdn_scan",
        compiler_params=_cparams(("arbitrary",)),
    )(x, qk, kdt, qg, eg, u, u, u, ng)


def _outproj_kernel(x_ref, mod_ref, yc_ref, ym_ref, yg_ref, w_ref, o_ref):
    parts = [yc_ref[0]] + [y_ref[0, p] for y_ref in (ym_ref, yg_ref) for p in range(3)]
    y = jnp.concatenate([t.astype(BF16) for t in parts], axis=-1)
    o_ref[0] = x_ref[0] + mod_ref[0][5:6] * _dot(y, w_ref[...])


def _outproj(x, mod, yc, ym, yg, w, *, tm):
    b, s, d = x.shape
    ch = yc.shape[-1]
    return pl.pallas_call(
        _outproj_kernel,
        grid=(b, s // tm),
        in_specs=[pl.BlockSpec((1, tm, d), lambda bi, i: (bi, i, 0)),
                  pl.BlockSpec((1, N_MOD, d), lambda bi, i: (bi, 0, 0)),
                  pl.BlockSpec((1, tm, ch), lambda bi, i: (bi, i, 0)),
                  pl.BlockSpec((1, 3, tm, LANES), lambda bi, i: (bi, 0, i, 0)),
                  pl.BlockSpec((1, 3, tm, LANES), lambda bi, i: (bi, 0, i, 0)),
                  pl.BlockSpec(w.shape, lambda bi, i: (0, 0), pipeline_mode=pl.Buffered(1))],
        out_specs=pl.BlockSpec((1, tm, d), lambda bi, i: (bi, i, 0)),
        out_shape=jax.ShapeDtypeStruct((b, s, d), F32),
        name="out_proj",
        compiler_params=_cparams(("arbitrary", "arbitrary")),
    )(x, mod, yc, ym, yg, w)


def _token_mixing(x, mod, ln_g, w_in, conv_w, conv_b, conv_ln_g, conv_ln_b, gdn_conv_w, gdn_a_log, gdn_dt_bias,
                  gdn_norm_g, w_out, tables, *, tm, ts_conv, ts_moba, tq_moba, ts_gdn, ts_scan):
    d, d_in = w_in.shape
    conv_ch = conv_w.shape[1]
    n_pad = -(-d_in // LANES) * LANES
    w_in_p = jnp.zeros((d, n_pad), BF16).at[:, :d_in].set(w_in.astype(BF16))
    u = _inproj(x, mod, ln_g, w_in_p, tm=tm)

    moba_w = 3 * LANES
    q_blk_moba = 2 * conv_ch // LANES
    q_blk_gdn = q_blk_moba + 3 * moba_w // LANES
    z_blk = q_blk_gdn + 9
    ab_blk = z_blk + 3

    y_conv = _conformer_conv(u, conv_w, conv_b, conv_ln_g, conv_ln_b, ts=ts_conv)

    qp, ka, va, km = _moba_prep(u, tables, q_lane_block=q_blk_moba, ts=ts_moba)
    y_moba = _moba_attention(qp, ka, va, _block_mean_rows(km), tq=tq_moba)

    xs, qk, kdt, qg, eg = _gdn_chunks(u, gdn_conv_w, gdn_a_log, gdn_dt_bias, q_lane_block=q_blk_gdn,
                                       ab_lane_block=ab_blk, ts=ts_gdn)
    y_gdn = _gdn_scan(xs, qk, kdt, qg, eg, u, gdn_norm_g, z_lane_block=z_blk, ts=ts_scan)

    return _outproj(x, mod, y_conv, y_moba, y_gdn, w_out.astype(BF16), tm=tm)


def kernel(x, c, w_ada, b_ada, ln_ffn1_g, ffn1_w_gate, ffn1_w_up, ffn1_w_down, ln_mix_g, w_in, conv_w, conv_b, conv_ln_g, conv_ln_b, gdn_conv_w, gdn_a_log, gdn_dt_bias, gdn_norm_g, w_out, ln_ffn2_g, ffn2_w_gate, ffn2_w_up, ffn2_w_down, final_g):
    depth = w_ada.shape[0]
    s = x.shape[1]
    tm = min(512, s)
    tiles = dict(tm=tm, ts_conv=min(512, s), ts_moba=min(1024, s), tq_moba=min(512, s), ts_gdn=min(1024, s),
                 ts_scan=min(512, s))
    mods = _ada_mod(c, w_ada, b_ada)
    tables = _rope_tables(s)
    for l in range(depth):
        mod = mods[l]
        x = _ffn(x, mod, ln_ffn1_g, ffn1_w_gate, ffn1_w_up, ffn1_w_down, final_g,
                 layer=l, mod_base=0, final=False, tm=tm)
        x = _token_mixing(x, mod, ln_mix_g[l], w_in[l], conv_w[l], conv_b[l], conv_ln_g[l], conv_ln_b[l],
                          gdn_conv_w[l], gdn_a_log[l], gdn_dt_bias[l], gdn_norm_g[l], w_out[l], tables, **tiles)
        x = _ffn(x, mod, ln_ffn2_g, ffn2_w_gate, ffn2_w_up, ffn2_w_down, final_g,
                 layer=l, mod_base=6, final=(l == depth - 1), tm=tm)
    return x
```

```python
import functools
import math

import numpy as np
import jax
import jax.numpy as jnp
from jax import lax
from jax.experimental import pallas as pl
from jax.experimental.pallas import tpu as pltpu

F32 = jnp.float32
BF16 = jnp.bfloat16

EPS = 1e-6
HEAD_DIM = 64
LANES = 128
SUBLANES = 8
N_MOD = 9
FFN_CHUNK = 256
CONV_K = 31
CONV_HALO = 32
MOBA_BLOCK = 256
MOBA_TOPK = 3
MOBA_GROUP = 4
ROPE_DIM = HEAD_DIM // 4
ROPE_THETA = 500000.0
GDN_CONV_K = 4
GDN_HALO = 8
GDN_CHUNK = 64
MASK_NEG = -(2.0 ** 100)
VMEM_LIMIT = 56 * 1024 * 1024


def _cparams(sem):
    return pltpu.CompilerParams(dimension_semantics=sem, vmem_limit_bytes=VMEM_LIMIT)


def _iota(shape, dim):
    return lax.broadcasted_iota(jnp.int32, shape, dim)


def _div64(t):
    return t >> 6


def _dot(a, b):
    return jnp.dot(a, b, preferred_element_type=F32)


def _dot_nt(a, b):
    return lax.dot_general(a, b, (((1,), (1,)), ((), ())), preferred_element_type=F32)


def _split3(x):
    x0 = x.astype(BF16)
    r1 = x - x0.astype(F32)
    x1 = r1.astype(BF16)
    x2 = (r1 - x1.astype(F32)).astype(BF16)
    return x0, x1, x2


def _dot_nt_split3(a, b):
    a0, a1, a2 = _split3(a)
    b0, b1, b2 = _split3(b)
    small = _dot_nt(a0, b2) + _dot_nt(a1, b1) + _dot_nt(a2, b0)
    mid = _dot_nt(a0, b1) + _dot_nt(a1, b0)
    return _dot_nt(a0, b0) + (mid + small)


def _dot_data_01(x, m01):
    x0, x1, x2 = _split3(x)
    return _dot(x0, m01) + (_dot(x1, m01) + _dot(x2, m01))


def _dot_01_data(m01, x):
    x0, x1, x2 = _split3(x)
    return _dot(m01, x0) + (_dot(m01, x1) + _dot(m01, x2))


def _dot_split2(a, b):
    a0 = a.astype(BF16)
    a1 = (a - a0.astype(F32)).astype(BF16)
    b0 = b.astype(BF16)
    b1 = (b - b0.astype(F32)).astype(BF16)
    return _dot(a0, b0) + (_dot(a0, b1) + _dot(a1, b0))


def _silu(x):
    return x * jax.nn.sigmoid(x)


def _modulate(x, g, shift, scale):
    ms = jnp.mean(x * x, axis=-1, keepdims=True)
    y = x * lax.rsqrt(ms + EPS) * g
    return y * (1.0 + scale) + shift


def _ada_kernel(c_ref, w_ref, b_ref, o_ref):
    ca = _silu(c_ref[...]).astype(BF16)
    o_ref[0] = _dot(ca, w_ref[0].astype(BF16)) + b_ref[0]


def _ada_mod(c, w_ada, b_ada):
    depth, d, nd = w_ada.shape
    b = c.shape[0]
    rows = 8
    c8 = jnp.zeros((rows, d), F32).at[:b].set(c)
    out = pl.pallas_call(
        _ada_kernel,
        grid=(depth, nd // d),
        in_specs=[pl.BlockSpec((rows, d), lambda l, j: (0, 0)),
                  pl.BlockSpec((1, d, d), lambda l, j: (l, 0, j)),
                  pl.BlockSpec((1, 1, d), lambda l, j: (l, 0, j))],
        out_specs=pl.BlockSpec((1, rows, d), lambda l, j: (l, 0, j)),
        out_shape=jax.ShapeDtypeStruct((depth, rows, nd), F32),
        name="ada_mod",
        compiler_params=_cparams(("arbitrary", "arbitrary")),
    )(c8, w_ada, b_ada.reshape(depth, 1, nd))
    return out[:, :b].reshape(depth, b, nd // d, d)


def _ffn_kernel(x_ref, mod_ref, g_ref, wg_ref, wu_ref, wd_ref, fg_ref, o_ref, *, mod_base, final):
    x = x_ref[0]
    m = mod_ref[0]
    shift, scale, gate = m[mod_base:mod_base + 1], m[mod_base + 1:mod_base + 2], m[mod_base + 2:mod_base + 3]
    h = _modulate(x, g_ref[0], shift, scale).astype(BF16)
    down = None
    for c0 in range(0, wg_ref.shape[2], FFN_CHUNK):
        a = _dot(h, wg_ref[0, :, c0:c0 + FFN_CHUNK].astype(BF16))
        u = _dot(h, wu_ref[0, :, c0:c0 + FFN_CHUNK].astype(BF16))
        act = (_silu(a) * u).astype(BF16)
        part = _dot(act, wd_ref[0, c0:c0 + FFN_CHUNK, :].astype(BF16))
        down = part if down is None else down + part
    y = x + (0.5 * gate) * down
    if final:
        y = y * lax.rsqrt(jnp.mean(y * y, axis=-1, keepdims=True) + EPS) * fg_ref[...]
    o_ref[0] = y


def _ffn(x, mod, g, wg, wu, wd, final_g, *, layer, mod_base, final, tm):
    b, s, d = x.shape
    depth, _, f = wg.shape
    resident = dict(pipeline_mode=pl.Buffered(1))
    return pl.pallas_call(
        functools.partial(_ffn_kernel, mod_base=mod_base, final=final),
        grid=(b, s // tm),
        in_specs=[pl.BlockSpec((1, tm, d), lambda bi, i: (bi, i, 0)),
                  pl.BlockSpec((1, N_MOD, d), lambda bi, i: (bi, 0, 0)),
                  pl.BlockSpec((1, 1, d), lambda bi, i: (layer, 0, 0)),
                  pl.BlockSpec((1, d, f), lambda bi, i: (layer, 0, 0), **resident),
                  pl.BlockSpec((1, d, f), lambda bi, i: (layer, 0, 0), **resident),
                  pl.BlockSpec((1, f, d), lambda bi, i: (layer, 0, 0), **resident),
                  pl.BlockSpec((1, d), lambda bi, i: (0, 0))],
        out_specs=pl.BlockSpec((1, tm, d), lambda bi, i: (bi, i, 0)),
        out_shape=jax.ShapeDtypeStruct((b, s, d), F32),
        name="ffn",
        compiler_params=_cparams(("arbitrary", "arbitrary")),
    )(x, mod, g.reshape(depth, 1, d), wg, wu, wd, final_g.reshape(1, d))


def _inproj_kernel(x_ref, mod_ref, g_ref, w_ref, c_ref, sa_ref, sb_ref, o_ref, qo_ref, ko_ref, vo_ref, km_ref,
                   *, moba_lo):
    i = pl.program_id(1)
    m = mod_ref[0]
    h = _modulate(x_ref[0], g_ref[...], m[3:4], m[4:5]).astype(BF16)
    u = _dot(h, w_ref[...])
    moba_hi = moba_lo + 9 * LANES
    o_ref[0] = jnp.concatenate([u[:, :moba_lo], u[:, moba_hi:]], axis=1)
    c, sa, sb = c_ref[...], sa_ref[...], sb_ref[...]
    first_blk = i * (u.shape[0] // MOBA_BLOCK)
    for p in range(3):
        q, k, v = (u[:, moba_lo + (3 * j + p) * LANES:moba_lo + (3 * j + p + 1) * LANES] for j in range(3))
        _moba_prep_pair(q, k, v, c, sa, sb, first_blk, p, qo_ref, ko_ref, vo_ref, km_ref)


def _inproj(x, mod, g, w, tables, *, moba_lo, tm):
    b, s, d = x.shape
    n = w.shape[1]
    n_rest = n - 9 * LANES
    nb = s // MOBA_BLOCK
    nbs = tm // MOBA_BLOCK
    tab = pl.BlockSpec((tm, LANES), lambda bi, i: (i, 0))
    hsp = pl.BlockSpec((1, 6, tm, LANES), lambda bi, i: (bi, 0, i, 0))
    head = lambda dt: jax.ShapeDtypeStruct((b, 6, s, LANES), dt)
    return pl.pallas_call(
        functools.partial(_inproj_kernel, moba_lo=moba_lo),
        grid=(b, s // tm),
        in_specs=[pl.BlockSpec((1, tm, d), lambda bi, i: (bi, i, 0)),
                  pl.BlockSpec((1, N_MOD, d), lambda bi, i: (bi, 0, 0)),
                  pl.BlockSpec((1, d), lambda bi, i: (0, 0)),
                  pl.BlockSpec((d, n), lambda bi, i: (0, 0), pipeline_mode=pl.Buffered(1)),
                  tab, tab, tab],
        out_specs=[pl.BlockSpec((1, tm, n_rest), lambda bi, i: (bi, i, 0)), hsp, hsp,
                   pl.BlockSpec((1, 6, LANES, tm), lambda bi, i: (bi, 0, 0, i)),
                   pl.BlockSpec((1, 6, nbs, 1, LANES), lambda bi, i: (bi, 0, i, 0, 0))],
        out_shape=[jax.ShapeDtypeStruct((b, s, n_rest), F32), head(F32), head(BF16),
                   jax.ShapeDtypeStruct((b, 6, LANES, s), BF16),
                   jax.ShapeDtypeStruct((b, 6, nb, 1, LANES), F32)],
        name="in_proj",
        compiler_params=_cparams(("arbitrary", "arbitrary")),
    )(x, mod, g.reshape(1, d), w, *tables)


def _conv_kernel(u_ref, p_ref, w_ref, b_ref, lg_ref, lb_ref, o_ref, hs_ref, sh_ref, *, ts, ch):
    i = pl.program_id(1)
    cur = u_ref[0]
    prev = p_ref[0]
    hprev = prev[:, :ch] * jax.nn.sigmoid(prev[:, ch:])
    hs_ref[0:CONV_HALO, :] = jnp.where(i > 0, hprev, 0.0)
    hs_ref[CONV_HALO:CONV_HALO + ts, :] = cur[:, :ch] * jax.nn.sigmoid(cur[:, ch:])
    hs_ref[CONV_HALO + ts:CONV_HALO + ts + SUBLANES, :] = jnp.zeros((SUBLANES, ch), F32)
    span = CONV_HALO + ts
    for r in range(SUBLANES):
        sh_ref[r] = hs_ref[pl.ds(r, span), :]
    base = CONV_HALO - (CONV_K - 1)
    acc = jnp.zeros((ts, ch), F32) + b_ref[...]
    for k in range(CONV_K):
        r = (base + k) % SUBLANES
        acc = acc + sh_ref[r, pl.ds(base + k - r, ts), :] * w_ref[k:k + 1, :]
    mu = jnp.mean(acc, axis=-1, keepdims=True)
    cen = acc - mu
    var = jnp.mean(cen * cen, axis=-1, keepdims=True)
    o_ref[0] = _silu(cen * lax.rsqrt(var + EPS) * lg_ref[...] + lb_ref[...])


def _conformer_conv(u, w, bias, ln_g, ln_b, *, ts):
    b, s, _ = u.shape
    ch = w.shape[1]
    hb = ts // CONV_HALO
    return pl.pallas_call(
        functools.partial(_conv_kernel, ts=ts, ch=ch),
        grid=(b, s // ts),
        in_specs=[pl.BlockSpec((1, ts, 2 * ch), lambda bi, i: (bi, i, 0)),
                  pl.BlockSpec((1, CONV_HALO, 2 * ch), lambda bi, i: (bi, jnp.maximum(i * hb - 1, 0), 0)),
                  pl.BlockSpec((CONV_K, ch), lambda bi, i: (0, 0)),
                  pl.BlockSpec((1, ch), lambda bi, i: (0, 0)),
                  pl.BlockSpec((1, ch), lambda bi, i: (0, 0)),
                  pl.BlockSpec((1, ch), lambda bi, i: (0, 0))],
        out_specs=pl.BlockSpec((1, ts, ch), lambda bi, i: (bi, i, 0)),
        out_shape=jax.ShapeDtypeStruct((b, s, ch), F32),
        scratch_shapes=[pltpu.VMEM((CONV_HALO + ts + SUBLANES, ch), F32),
                        pltpu.VMEM((SUBLANES, CONV_HALO + ts, ch), F32)],
        name="conformer_conv",
        compiler_params=_cparams(("arbitrary", "arbitrary")),
    )(u, u, w, bias.reshape(1, ch), ln_g.reshape(1, ch), ln_b.reshape(1, ch))


def _rope_tables(s):
    half = ROPE_DIM // 2
    inv = jnp.exp(-math.log(ROPE_THETA) * jnp.arange(0, ROPE_DIM, 2, dtype=F32) / ROPE_DIM)
    ang = jnp.arange(s, dtype=F32)[:, None] * inv[None, :]
    cs = jnp.concatenate([jnp.cos(ang), jnp.sin(ang), jnp.ones((s, 1), F32)], axis=-1)
    d = np.arange(LANES) % HEAD_DIM
    rows = 2 * half + 1
    mc, msa, msb = (np.zeros((rows, LANES), np.float32) for _ in range(3))
    for lane_i, di in enumerate(d):
        if di < ROPE_DIM:
            mc[di % half, lane_i] = 1.0
        else:
            mc[2 * half, lane_i] = 1.0
        if di < half:
            msa[half + di, lane_i] = -1.0
        elif di < ROPE_DIM:
            msb[half + (di - half), lane_i] = 1.0
    spread = lambda m: jnp.dot(cs, jnp.asarray(m), precision=lax.Precision.HIGHEST)
    return spread(mc), spread(msa), spread(msb)


def _mask_base(hh):
    return (1 - hh) * HEAD_DIM


def _moba_prep_pair(q, k, v, c, sa, sb, first_blk, p, qo_ref, ko_ref, vo_ref, km_ref):
    half = ROPE_DIM // 2

    def rope(x):
        return x * c + pltpu.roll(x, LANES - half, 1) * sa + pltpu.roll(x, half, 1) * sb

    qr, kr = rope(q), rope(k)
    lane = _iota(qr.shape, 1)
    nbs = qr.shape[0] // MOBA_BLOCK
    blk_of_row = first_blk + (_iota(qr.shape, 0) >> (MOBA_BLOCK.bit_length() - 1))
    for hh in range(2):
        own = (lane >= hh * HEAD_DIM) & (lane < (hh + 1) * HEAD_DIM)
        mb = _mask_base(hh)
        kz = jnp.where(own, kr, 0.0)
        qo_ref[0, 2 * p + hh] = jnp.where(own, qr, 0.0)
        ko_ref[0, 2 * p + hh] = jnp.where(own, kr, jnp.where(lane == mb + blk_of_row, 1.0, 0.0)).astype(BF16)
        vo_ref[0, 2 * p + hh] = jnp.where(own, v, jnp.where(lane == mb, 1.0, 0.0)).T.astype(BF16)
        for j in range(nbs):
            km_ref[0, 2 * p + hh, j] = jnp.mean(kz[j * MOBA_BLOCK:(j + 1) * MOBA_BLOCK], axis=0, keepdims=True)


def _block_mean_rows(km):
    b, h, nb = km.shape[:3]
    km = km[:, :, :, 0, :].reshape(b, h // 2, 2, nb, LANES)
    rows = [jnp.zeros((b, h // 2, LANES, LANES), F32).at[:, :, _mask_base(hh):_mask_base(hh) + nb, :].set(km[:, :, hh])
            for hh in range(2)]
    return jnp.stack(rows, axis=2).reshape(b, h, LANES, LANES)


def _moba_kernel(q_ref, ka_ref, vat_ref, kmt_ref, o_ref, sa_ref, sb_ref, m_ref, acc_ref, *, tq):
    i = pl.program_id(2)
    blk_shift = MOBA_BLOCK.bit_length() - 1
    gk = MOBA_GROUP * MOBA_BLOCK
    first_blk = i * (tq // MOBA_BLOCK)
    lane = _iota((tq, LANES), 1)
    nsel = LANES // 4
    blk_id = _iota((nsel, tq), 0)
    blk_f = blk_id.astype(F32)
    own_blk = first_blk + (_iota((nsel, tq), 1) >> blk_shift)
    neg_inf = -jnp.inf

    qas = []
    for hh in range(2):
        mb = _mask_base(hh)
        qp = q_ref[0, hh]
        gate = _dot_nt_split3(kmt_ref[0, hh], qp)[mb:mb + nsel, :]
        s = jnp.where(blk_id < own_blk, gate, neg_inf)
        sel = jnp.where(blk_id == own_blk, 1.0, 0.0)
        for _ in range(MOBA_TOPK):
            m = jnp.max(s, axis=0, keepdims=True)
            idx = jnp.min(jnp.where(s == m, blk_f, float(LANES)), axis=0, keepdims=True)
            hit = blk_f == idx
            sel = jnp.where(hit, 1.0, sel)
            s = jnp.where(hit, neg_inf, s)
        neg_t = jnp.where(sel > 0.5, 0.0, MASK_NEG)
        pad_lo = jnp.zeros((mb, tq), F32)
        pad_hi = jnp.zeros((LANES - mb - nsel, tq), F32)
        negm_t = jnp.concatenate([t for t in (pad_lo, neg_t, pad_hi) if t.shape[0]], axis=0)
        qas.append((qp.T * (HEAD_DIM ** -0.5 * math.log2(math.e)) + negm_t).astype(BF16))

    def scores(buf, g):
        start = pl.multiple_of(g * gk, gk)
        for hh in range(2):
            buf[hh] = _dot(ka_ref[0, hh, pl.ds(start, gk), :], qas[hh])

    def consume(buf, g, causal):
        start = pl.multiple_of(g * gk, gk)
        for hh in range(2):
            s = buf[hh]
            if causal:
                s = jnp.where(start + _iota((gk, tq), 0) <= i * tq + _iota((gk, tq), 1), s, MASK_NEG)
            m = m_ref[hh]
            mn = jnp.maximum(m, jnp.max(s, axis=0, keepdims=True))
            p = jnp.exp2(s - mn).astype(BF16)
            pv = _dot(vat_ref[0, hh, :, pl.ds(start, gk)], p)
            m_ref[hh] = mn
            acc_ref[hh] = jnp.exp2(m - mn) * acc_ref[hh] + pv

    m_ref[...] = jnp.full(m_ref.shape, neg_inf, F32)
    acc_ref[...] = jnp.zeros(acc_ref.shape, F32)
    n_past = first_blk >> (MOBA_GROUP.bit_length() - 1)
    scores(sa_ref, 0)

    def body(j, carry):
        g = 2 * j
        scores(sb_ref, g + 1)
        consume(sa_ref, g, False)
        scores(sa_ref, g + 2)
        consume(sb_ref, g + 1, False)
        return carry

    lax.fori_loop(0, n_past >> 1, body, 0)
    odd = (n_past & 1) == 1

    @pl.when(odd)
    def _():
        scores(sb_ref, n_past)
        consume(sa_ref, n_past - 1, False)
        consume(sb_ref, n_past, True)

    @pl.when(jnp.logical_not(odd))
    def _():
        consume(sa_ref, n_past, True)

    l0, l1 = _mask_base(0), _mask_base(1)
    acc0, acc1 = acc_ref[0], acc_ref[1]
    o0 = (acc0 / acc0[l0:l0 + 1, :]).T
    o1 = (acc1 / acc1[l1:l1 + 1, :]).T
    o_ref[0, 0] = jnp.where(lane < HEAD_DIM, o0, o1)


def _moba_attention(qp, ka, vat, kmt, *, tq):
    b, _, s, _ = qp.shape
    assert s % (MOBA_GROUP * MOBA_BLOCK) == 0 and (MOBA_GROUP * MOBA_BLOCK) % tq == 0 and tq % MOBA_BLOCK == 0
    assert s // MOBA_BLOCK <= LANES // 4
    blk = tq
    gk = MOBA_GROUP * MOBA_BLOCK
    return pl.pallas_call(
        functools.partial(_moba_kernel, tq=tq),
        grid=(b, 3, s // tq),
        in_specs=[pl.BlockSpec((1, 2, blk, LANES), lambda bi, p, i: (bi, p, i, 0)),
                  pl.BlockSpec((1, 2, s, LANES), lambda bi, p, i: (bi, p, 0, 0)),
                  pl.BlockSpec((1, 2, LANES, s), lambda bi, p, i: (bi, p, 0, 0)),
                  pl.BlockSpec((1, 2, LANES, LANES), lambda bi, p, i: (bi, p, 0, 0))],
        out_specs=pl.BlockSpec((1, 1, blk, LANES), lambda bi, p, i: (bi, p, i, 0)),
        out_shape=jax.ShapeDtypeStruct((b, 3, s, LANES), F32),
        scratch_shapes=[pltpu.VMEM((2, gk, tq), F32), pltpu.VMEM((2, gk, tq), F32),
                        pltpu.VMEM((2, 1, tq), F32), pltpu.VMEM((2, LANES, tq), F32)],
        name="moba_attention",
        compiler_params=_cparams(("arbitrary", "arbitrary", "arbitrary")),
    )(qp, ka, vat, kmt)


def _gdn_chunks_kernel(q_ref, k_ref, v_ref, hq_ref, hk_ref, hv_ref, ab_ref, wq_ref, wk_ref, wv_ref, al_ref, dt_ref,
                       x_ref, qk_ref, kdt_ref, qg_ref, eg_ref, xs_ref, *, ts):
    i = pl.program_id(1)
    p = pl.program_id(2)
    base = GDN_HALO - (GDN_CONV_K - 1)

    def conv_silu(x_ref, h_ref, w_ref):
        xs_ref[0:GDN_HALO, :] = jnp.where(i > 0, h_ref[0], 0.0)
        xs_ref[GDN_HALO:GDN_HALO + ts, :] = x_ref[0]
        acc = jnp.zeros((ts, LANES), F32)
        for k in range(GDN_CONV_K):
            acc = acc + xs_ref[pl.ds(base + k, ts), :] * w_ref[k:k + 1, :]
        return _silu(acc)

    q, k, v = conv_silu(q_ref, hq_ref, wq_ref), conv_silu(k_ref, hk_ref, wk_ref), conv_silu(v_ref, hv_ref, wv_ref)

    r2 = _iota((2 * LANES, 2 * LANES), 0)
    c2 = _iota((2 * LANES, 2 * LANES), 1)
    head_sum = jnp.where(_div64(r2) == _div64(c2), 1.0, 0.0).astype(BF16)
    ssq = _dot_data_01(jnp.concatenate([q * q, k * k], axis=1), head_sum)
    qn = q * lax.rsqrt(ssq[:, :LANES] + EPS) * (HEAD_DIM ** -0.5)
    kn = k * lax.rsqrt(ssq[:, LANES:] + EPS)

    ab = ab_ref[0]
    lane = _iota(ab.shape, 1)
    nh = 6
    xa = ab + dt_ref[...]
    softplus = jnp.maximum(xa, 0.0) + jnp.log1p(jnp.exp(-jnp.abs(xa)))
    g_beta = jnp.where(lane < nh, -jnp.exp(al_ref[...]) * softplus,
                       jnp.where(lane < 2 * nh, jax.nn.sigmoid(ab), 0.0))
    pr = _iota((LANES, 2 * LANES), 0)
    pc = _iota((LANES, 2 * LANES), 1)
    pick = jnp.where(pr == 2 * p + _div64(pc & (LANES - 1)) + jnp.where(pc < LANES, 0, nh), 1.0, 0.0).astype(BF16)
    gbx = _dot_data_01(g_beta, pick)
    gx, bx = gbx[:, :LANES], gbx[:, LANES:]

    sub = min(ts, 2 * LANES)
    tr = _iota((sub, sub), 0)
    tc = _iota((sub, sub), 1)
    in_chunk_tril = jnp.where((_div64(tr) == _div64(tc)) & (tc <= tr), 1.0, 0.0).astype(BF16)
    gcx = jnp.concatenate([_dot_01_data(in_chunk_tril, gx[r0:r0 + sub]) for r0 in range(0, ts, sub)],
                          axis=0)
    nck = ts // GDN_CHUNK
    last = gcx.reshape(nck, GDN_CHUNK, LANES)[:, GDN_CHUNK - 1:GDN_CHUNK, :]
    glx = jnp.broadcast_to(last, (nck, GDN_CHUNK, LANES)).reshape(ts, LANES)
    eg = jnp.exp(gcx)

    kb = kn * bx
    qg_ref[0, 0] = (qn * eg).astype(BF16)
    eg_ref[0, 0] = jnp.exp(glx)
    _gdn_solve_tile(qn.astype(BF16), kn.astype(BF16), kb.astype(BF16), v * bx, kb * eg, kn * jnp.exp(glx - gcx), gcx,
                    x_ref, qk_ref, kdt_ref)


def _gdn_chunks(u, conv_w, a_log, dt_bias, *, q_lane_block, ab_lane_block, ts):
    b, s, _ = u.shape
    qb = q_lane_block
    hb = ts // GDN_HALO
    nh = a_log.shape[0]
    pad = lambda t: jnp.zeros((1, LANES), F32).at[0, :nh].set(t)
    cur = lambda off: pl.BlockSpec((1, ts, LANES), lambda bi, i, p: (bi, i, qb + off + p))
    halo = lambda off: pl.BlockSpec((1, GDN_HALO, LANES), lambda bi, i, p: (bi, jnp.maximum(i * hb - 1, 0), qb + off + p))
    wsp = lambda off: pl.BlockSpec((GDN_CONV_K, LANES), lambda bi, i, p: (0, off + p))
    vec = pl.BlockSpec((1, LANES), lambda bi, i, p: (0, 0))
    one = pl.BlockSpec((1, 1, ts, LANES), lambda bi, i, p: (bi, p, i, 0))
    two = pl.BlockSpec((1, 1, 2 * ts, LANES), lambda bi, i, p: (bi, p, i, 0))
    sh1 = lambda dt: jax.ShapeDtypeStruct((b, 3, s, LANES), dt)
    sh2 = lambda dt: jax.ShapeDtypeStruct((b, 3, 2 * s, LANES), dt)
    return pl.pallas_call(
        functools.partial(_gdn_chunks_kernel, ts=ts),
        grid=(b, s // ts, 3),
        in_specs=[cur(0), cur(3), cur(6), halo(0), halo(3), halo(6),
                  pl.BlockSpec((1, ts, LANES), lambda bi, i, p: (bi, i, ab_lane_block)),
                  wsp(0), wsp(3), wsp(6), vec, vec],
        out_specs=[two, two, two, one, one],
        out_shape=[sh2(F32), sh2(BF16), sh2(BF16), sh1(BF16), sh1(F32)],
        scratch_shapes=[pltpu.VMEM((GDN_HALO + ts, LANES), F32)],
        name="gdn_chunks",
        compiler_params=_cparams(("arbitrary", "arbitrary", "arbitrary")),
    )(u, u, u, u, u, u, u, conv_w, conv_w, conv_w, pad(a_log), pad(dt_bias))


def _stack2(top, bot):
    return jnp.concatenate([top, bot], axis=0)


def _gdn_solve_tile(qn_t, kn_t, kb_t, vb_t, kbg_t, kd_t, gc_t, x_ref, qk_ref, kdt_ref):
    cs = GDN_CHUNK
    n2 = 2 * cs
    chunks = qn_t.shape[0] // cs
    lane = _iota((cs, LANES), 1)
    low = lane < HEAD_DIM
    r = _iota((n2, n2), 0)
    c = _iota((n2, n2), 1)
    bd = _div64(r) == _div64(c)
    eye = jnp.where(r == c, 1.0, 0.0)
    diag8 = (r >> 3) == (c >> 3)
    merge_masks = [((r >> (sh + 1)) == (c >> (sh + 1))) & (((r >> sh) & 1) == 1) & (((c >> sh) & 1) == 0)
                   for sh in (3, 4, 5)]

    def per_head_rows(t, fill=0.0):
        return _stack2(jnp.where(low, t, fill), jnp.where(low, fill, t))

    cis = range(chunks)
    rows = [slice(ci * cs, (ci + 1) * cs) for ci in cis]
    rows2 = [pl.ds(ci * n2, n2) for ci in cis]
    kn = [per_head_rows(kn_t[rw]) for rw in rows]
    kb = [per_head_rows(kb_t[rw]) for rw in rows]
    qn = [per_head_rows(qn_t[rw]) for rw in rows]
    kk = [_dot_nt(kb[ci], kn[ci]) for ci in cis]
    qk = [_dot_nt(qn[ci], kn[ci]) for ci in cis]
    decay = []
    for rw in rows:
        gc = gc_t[rw]
        gsw = pltpu.roll(gc, HEAD_DIM, 1)
        gcol = _stack2(jnp.where(low, gc, gsw), jnp.where(low, gsw, gc))
        decay.append(jnp.exp(jnp.minimum(gcol - gcol.T, 0.0)))
    lmat = [jnp.where(c < r, kk[ci] * decay[ci], 0.0) for ci in cis]
    for ci in cis:
        qk_ref[0, 0, rows2[ci], :] = jnp.where(c <= r, qk[ci] * decay[ci], 0.0).astype(BF16)

    ld = [jnp.where(diag8, lm, 0.0) for lm in lmat]
    ld2 = [_dot_split2(m, m) for m in ld]
    ld4 = [_dot_split2(m, m) for m in ld2]
    t = [eye - m for m in ld]
    t = [t[ci] + _dot_split2(t[ci], ld2[ci]) for ci in cis]
    t = [t[ci] + _dot_split2(t[ci], ld4[ci]) for ci in cis]
    for lower_left in merge_masks:
        tb = [m.astype(BF16) for m in t]
        inner = [_dot(jnp.where(lower_left, lmat[ci], 0.0).astype(BF16), tb[ci]) for ci in cis]
        t = [t[ci] - _dot(tb[ci], inner[ci].astype(BF16)) for ci in cis]
    for ci in cis:
        kbg = kbg_t[rows[ci]]
        vsw = pltpu.roll(vb_t[rows[ci]], HEAD_DIM, 1)
        x = jnp.where(bd, _stack2(kbg, kbg), _stack2(vsw, vsw))
        x_ref[0, 0, rows2[ci], :] = _dot(t[ci].astype(BF16), x.astype(BF16))
    for ci in cis:
        kd = kd_t[rows[ci]]
        kdt_ref[0, 0, rows2[ci], :] = jnp.where(bd, _stack2(kd, kd), 0.0).T.astype(BF16)


def _gdn_scan_kernel(x_ref, qk_ref, kdt_ref, qg_ref, eg_ref, z0_ref, z1_ref, z2_ref, ng_ref, o_ref, st_ref,
                     *, chunks, batch):
    cs = GDN_CHUNK
    n2 = 2 * cs
    r = _iota((n2, n2), 0)
    c = _iota((n2, n2), 1)
    bd = _div64(r) == _div64(c)
    hr = _iota((LANES, LANES), 0)
    hc = _iota((LANES, LANES), 1)
    head_mean = jnp.where(_div64(hr) == _div64(hc), 1.0 / HEAD_DIM, 0.0).astype(BF16)
    z_refs = (z0_ref, z1_ref, z2_ref)

    @pl.when(pl.program_id(0) == 0)
    def _():
        st_ref[...] = jnp.zeros_like(st_ref)

    def body(ci, carry):
        start = pl.multiple_of(ci * cs, cs)
        rows = pl.ds(start, cs)
        rows2 = pl.ds(pl.multiple_of(ci * n2, n2), n2)
        chains = [(bi, p) for bi in range(batch) for p in range(3)]
        ks = range(len(chains))
        st = [st_ref[bi * 3 + p] for bi, p in chains]
        stb = [t.astype(BF16) for t in st]
        x = [x_ref[bi, p, rows2, :] for bi, p in chains]
        ws = [_dot(jnp.where(bd, x[k], 0.0).astype(BF16), stb[k]) for k in ks]
        vnb = [(jnp.where(bd, 0.0, x[k]) - ws[k]).astype(BF16) for k in ks]
        qbd = [jnp.where(bd, _stack2(qg, qg), jnp.zeros((), BF16)) for qg in (qg_ref[bi, p, rows, :] for bi, p in chains)]
        o = [_dot(qbd[k], stb[k]) + _dot(qk_ref[bi, p, rows2, :], vnb[k]) for k, (bi, p) in enumerate(chains)]
        upd = [_dot(kdt_ref[bi, p, rows2, :], vnb[k]) for k, (bi, p) in enumerate(chains)]
        for k, (bi, p) in enumerate(chains):
            eg = pltpu.roll(eg_ref[bi, p, pl.ds(start, 8), :], HEAD_DIM, 1)[0:1, :]
            st_ref[bi * 3 + p] = st[k] * eg + upd[k]
        for k, (bi, p) in enumerate(chains):
            t = o[k]
            o_ref[bi, p, rows, :] = pltpu.roll(t[:cs] + t[cs:], HEAD_DIM, 1)
        return carry

    lax.fori_loop(0, chunks, body, 0)

    for bi in range(batch):
        for p in range(3):
            op = o_ref[bi, p]
            ms = _dot_data_01(op * op, head_mean)
            o_ref[bi, p] = op * lax.rsqrt(ms + EPS) * ng_ref[...] * _silu(z_refs[p][bi])


def _gdn_scan(x, qk, kdt, qg, eg, u, norm_g, *, z_lane_block, ts):
    b, _, s, _ = qg.shape
    chunks = ts // GDN_CHUNK
    zb = z_lane_block
    two = pl.BlockSpec((b, 3, 2 * ts, LANES), lambda i: (0, 0, i, 0))
    one = pl.BlockSpec((b, 3, ts, LANES), lambda i: (0, 0, i, 0))
    zsp = lambda p: pl.BlockSpec((b, ts, LANES), lambda i: (0, i, zb + p))
    ng = jnp.concatenate([norm_g, norm_g]).reshape(1, LANES)
    return pl.pallas_call(
        functools.partial(_gdn_scan_kernel, chunks=chunks, batch=b),
        grid=(s // ts,),
        in_specs=[two, two, two, one, one, zsp(0), zsp(1), zsp(2), pl.BlockSpec((1, LANES), lambda i: (0, 0))],
        out_specs=one,
        out_shape=jax.ShapeDtypeStruct((b, 3, s, LANES), F32),
        scratch_shapes=[pltpu.VMEM((b * 3, LANES, LANES), F32)],
        name="gdn_scan",
        compiler_params=_cparams(("arbitrary",)),
    )(x, qk, kdt, qg, eg, u, u, u, ng)


def _outproj_kernel(x_ref, mod_ref, yc_ref, ym_ref, yg_ref, w_ref, o_ref):
    parts = [yc_ref[0]] + [y_ref[0, p] for y_ref in (ym_ref, yg_ref) for p in range(3)]
    y = jnp.concatenate([t.astype(BF16) for t in parts], axis=-1)
    o_ref[0] = x_ref[0] + mod_ref[0][5:6] * _dot(y, w_ref[...])


def _outproj(x, mod, yc, ym, yg, w, *, tm):
    b, s, d = x.shape
    ch = yc.shape[-1]
    return pl.pallas_call(
        _outproj_kernel,
        grid=(b, s // tm),
        in_specs=[pl.BlockSpec((1, tm, d), lambda bi, i: (bi, i, 0)),
                  pl.BlockSpec((1, N_MOD, d), lambda bi, i: (bi, 0, 0)),
                  pl.BlockSpec((1, tm, ch), lambda bi, i: (bi, i, 0)),
                  pl.BlockSpec((1, 3, tm, LANES), lambda bi, i: (bi, 0, i, 0)),
                  pl.BlockSpec((1, 3, tm, LANES), lambda bi, i: (bi, 0, i, 0)),
                  pl.BlockSpec(w.shape, lambda bi, i: (0, 0), pipeline_mode=pl.Buffered(1))],
        out_specs=pl.BlockSpec((1, tm, d), lambda bi, i: (bi, i, 0)),
        out_shape=jax.ShapeDtypeStruct((b, s, d), F32),
        name="out_proj",
        compiler_params=_cparams(("arbitrary", "arbitrary")),
    )(x, mod, yc, ym, yg, w)


def _token_mixing(x, mod, ln_g, w_in, conv_w, conv_b, conv_ln_g, conv_ln_b, gdn_conv_w, gdn_a_log, gdn_dt_bias,
                  gdn_norm_g, w_out, tables, *, tm, ts_conv, tq_moba, ts_gdn, ts_scan):
    d, d_in = w_in.shape
    conv_ch = conv_w.shape[1]
    n_pad = -(-d_in // LANES) * LANES
    w_in_p = jnp.zeros((d, n_pad), BF16).at[:, :d_in].set(w_in.astype(BF16))
    u, qp, ka, va, km = _inproj(x, mod, ln_g, w_in_p, tables, moba_lo=2 * conv_ch, tm=tm)

    q_blk_gdn = 2 * conv_ch // LANES
    z_blk = q_blk_gdn + 9
    ab_blk = z_blk + 3

    y_conv = _conformer_conv(u, conv_w, conv_b, conv_ln_g, conv_ln_b, ts=ts_conv)
    y_moba = _moba_attention(qp, ka, va, _block_mean_rows(km), tq=tq_moba)

    xs, qk, kdt, qg, eg = _gdn_chunks(u, gdn_conv_w, gdn_a_log, gdn_dt_bias, q_lane_block=q_blk_gdn,
                                       ab_lane_block=ab_blk, ts=ts_gdn)
    y_gdn = _gdn_scan(xs, qk, kdt, qg, eg, u, gdn_norm_g, z_lane_block=z_blk, ts=ts_scan)

    return _outproj(x, mod, y_conv, y_moba, y_gdn, w_out.astype(BF16), tm=tm)


def kernel(x, c, w_ada, b_ada, ln_ffn1_g, ffn1_w_gate, ffn1_w_up, ffn1_w_down, ln_mix_g, w_in, conv_w, conv_b, conv_ln_g, conv_ln_b, gdn_conv_w, gdn_a_log, gdn_dt_bias, gdn_norm_g, w_out, ln_ffn2_g, ffn2_w_gate, ffn2_w_up, ffn2_w_down, final_g):
    depth = w_ada.shape[0]
    s = x.shape[1]
    tm = min(512, s)
    tiles = dict(tm=tm, ts_conv=min(512, s), tq_moba=min(512, s), ts_gdn=min(1024, s),
                 ts_scan=min(512, s))
    mods = _ada_mod(c, w_ada, b_ada)
    tables = _rope_tables(s)
    for l in range(depth):
        mod = mods[l]
        x = _ffn(x, mod, ln_ffn1_g, ffn1_w_gate, ffn1_w_up, ffn1_w_down, final_g,
                 layer=l, mod_base=0, final=False, tm=tm)
        x = _token_mixing(x, mod, ln_mix_g[l], w_in[l], conv_w[l], conv_b[l], conv_ln_g[l], conv_ln_b[l],
                          gdn_conv_w[l], gdn_a_log[l], gdn_dt_bias[l], gdn_norm_g[l], w_out[l], tables, **tiles)
        x = _ffn(x, mod, ln_ffn2_g, ffn2_w_gate, ffn2_w_up, ffn2_w_down, final_g,
                 layer=l, mod_base=6, final=(l == depth - 1), tm=tm)
    return x
```

```python
import functools
import math

import numpy as np
import jax
import jax.numpy as jnp
from jax import lax
from jax.experimental import pallas as pl
from jax.experimental.pallas import tpu as pltpu

F32 = jnp.float32
BF16 = jnp.bfloat16

EPS = 1e-6
HEAD_DIM = 64
LANES = 128
SUBLANES = 8
N_MOD = 9
FFN_CHUNK = 256
CONV_K = 31
CONV_HALO = 32
MOBA_BLOCK = 256
MOBA_TOPK = 3
MOBA_GROUP = 4
ROPE_DIM = HEAD_DIM // 4
ROPE_THETA = 500000.0
GDN_CONV_K = 4
GDN_HALO = 8
GDN_CHUNK = 64
MASK_NEG = -(2.0 ** 100)
VMEM_LIMIT = 56 * 1024 * 1024


def _cparams(sem):
    return pltpu.CompilerParams(dimension_semantics=sem, vmem_limit_bytes=VMEM_LIMIT)


def _iota(shape, dim):
    return lax.broadcasted_iota(jnp.int32, shape, dim)


def _div64(t):
    return t >> 6


def _dot(a, b):
    return jnp.dot(a, b, preferred_element_type=F32)


def _dot_nt(a, b):
    return lax.dot_general(a, b, (((1,), (1,)), ((), ())), preferred_element_type=F32)


def _split3(x):
    x0 = x.astype(BF16)
    r1 = x - x0.astype(F32)
    x1 = r1.astype(BF16)
    x2 = (r1 - x1.astype(F32)).astype(BF16)
    return x0, x1, x2


def _dot_nt_split3(a, b):
    a0, a1, a2 = _split3(a)
    b0, b1, b2 = _split3(b)
    small = _dot_nt(a0, b2) + _dot_nt(a1, b1) + _dot_nt(a2, b0)
    mid = _dot_nt(a0, b1) + _dot_nt(a1, b0)
    return _dot_nt(a0, b0) + (mid + small)


def _dot_data_01(x, m01):
    x0, x1, x2 = _split3(x)
    return _dot(x0, m01) + (_dot(x1, m01) + _dot(x2, m01))


def _dot_01_data(m01, x):
    x0, x1, x2 = _split3(x)
    return _dot(m01, x0) + (_dot(m01, x1) + _dot(m01, x2))


def _dot_split2(a, b):
    a0 = a.astype(BF16)
    a1 = (a - a0.astype(F32)).astype(BF16)
    b0 = b.astype(BF16)
    b1 = (b - b0.astype(F32)).astype(BF16)
    return _dot(a0, b0) + (_dot(a0, b1) + _dot(a1, b0))


def _silu(x):
    return x * jax.nn.sigmoid(x)


def _modulate(x, g, shift, scale):
    ms = jnp.mean(x * x, axis=-1, keepdims=True)
    y = x * lax.rsqrt(ms + EPS) * g
    return y * (1.0 + scale) + shift


def _ada_kernel(c_ref, w_ref, b_ref, o_ref):
    ca = _silu(c_ref[...]).astype(BF16)
    o_ref[0] = _dot(ca, w_ref[0].astype(BF16)) + b_ref[0]


def _ada_mod(c, w_ada, b_ada):
    depth, d, nd = w_ada.shape
    b = c.shape[0]
    rows = 8
    c8 = jnp.zeros((rows, d), F32).at[:b].set(c)
    out = pl.pallas_call(
        _ada_kernel,
        grid=(depth, nd // d),
        in_specs=[pl.BlockSpec((rows, d), lambda l, j: (0, 0)),
                  pl.BlockSpec((1, d, d), lambda l, j: (l, 0, j)),
                  pl.BlockSpec((1, 1, d), lambda l, j: (l, 0, j))],
        out_specs=pl.BlockSpec((1, rows, d), lambda l, j: (l, 0, j)),
        out_shape=jax.ShapeDtypeStruct((depth, rows, nd), F32),
        name="ada_mod",
        compiler_params=_cparams(("arbitrary", "arbitrary")),
    )(c8, w_ada, b_ada.reshape(depth, 1, nd))
    return out[:, :b].reshape(depth, b, nd // d, d)


def _ffn_kernel(x_ref, mod_ref, g_ref, wg_ref, wu_ref, wd_ref, fg_ref, o_ref, *, mod_base, final):
    x = x_ref[0]
    m = mod_ref[0]
    shift, scale, gate = m[mod_base:mod_base + 1], m[mod_base + 1:mod_base + 2], m[mod_base + 2:mod_base + 3]
    h = _modulate(x, g_ref[0], shift, scale).astype(BF16)
    down = None
    for c0 in range(0, wg_ref.shape[2], FFN_CHUNK):
        a = _dot(h, wg_ref[0, :, c0:c0 + FFN_CHUNK].astype(BF16))
        u = _dot(h, wu_ref[0, :, c0:c0 + FFN_CHUNK].astype(BF16))
        act = (_silu(a) * u).astype(BF16)
        part = _dot(act, wd_ref[0, c0:c0 + FFN_CHUNK, :].astype(BF16))
        down = part if down is None else down + part
    y = x + (0.5 * gate) * down
    if final:
        y = y * lax.rsqrt(jnp.mean(y * y, axis=-1, keepdims=True) + EPS) * fg_ref[...]
    o_ref[0] = y


def _ffn(x, mod, g, wg, wu, wd, final_g, *, layer, mod_base, final, tm):
    b, s, d = x.shape
    depth, _, f = wg.shape
    resident = dict(pipeline_mode=pl.Buffered(1))
    return pl.pallas_call(
        functools.partial(_ffn_kernel, mod_base=mod_base, final=final),
        grid=(b, s // tm),
        in_specs=[pl.BlockSpec((1, tm, d), lambda bi, i: (bi, i, 0)),
                  pl.BlockSpec((1, N_MOD, d), lambda bi, i: (bi, 0, 0)),
                  pl.BlockSpec((1, 1, d), lambda bi, i: (layer, 0, 0)),
                  pl.BlockSpec((1, d, f), lambda bi, i: (layer, 0, 0), **resident),
                  pl.BlockSpec((1, d, f), lambda bi, i: (layer, 0, 0), **resident),
                  pl.BlockSpec((1, f, d), lambda bi, i: (layer, 0, 0), **resident),
                  pl.BlockSpec((1, d), lambda bi, i: (0, 0))],
        out_specs=pl.BlockSpec((1, tm, d), lambda bi, i: (bi, i, 0)),
        out_shape=jax.ShapeDtypeStruct((b, s, d), F32),
        name="ffn",
        compiler_params=_cparams(("arbitrary", "arbitrary")),
    )(x, mod, g.reshape(depth, 1, d), wg, wu, wd, final_g.reshape(1, d))


def _inproj_kernel(x_ref, mod_ref, g_ref, w_ref, c_ref, sa_ref, sb_ref, cw_ref, cb_ref, clg_ref, clb_ref,
                   o_ref, yc_ref, qo_ref, ko_ref, vo_ref, km_ref, halo_ref, hs_ref, sh_ref, *, moba_lo):
    i = pl.program_id(1)
    m = mod_ref[0]
    h = _modulate(x_ref[0], g_ref[...], m[3:4], m[4:5]).astype(BF16)
    u = _dot(h, w_ref[...])
    moba_hi = moba_lo + 9 * LANES
    o_ref[0] = u[:, moba_hi:]
    yc_ref[0] = _conv_tile(u[:, :moba_lo], i == 0, cw_ref, cb_ref, clg_ref, clb_ref,
                           halo_ref, hs_ref, sh_ref).astype(yc_ref.dtype)
    c, sa, sb = c_ref[...], sa_ref[...], sb_ref[...]
    first_blk = i * (u.shape[0] // MOBA_BLOCK)
    for p in range(3):
        q, k, v = (u[:, moba_lo + (3 * j + p) * LANES:moba_lo + (3 * j + p + 1) * LANES] for j in range(3))
        _moba_prep_pair(q, k, v, c, sa, sb, first_blk, p, qo_ref, ko_ref, vo_ref, km_ref)


def _inproj(x, mod, g, w, tables, conv_w, conv_b, conv_ln_g, conv_ln_b, *, tm):
    b, s, d = x.shape
    n = w.shape[1]
    ch = conv_w.shape[1]
    moba_lo = 2 * ch
    n_rest = n - moba_lo - 9 * LANES
    nb = s // MOBA_BLOCK
    nbs = tm // MOBA_BLOCK
    tab = pl.BlockSpec((tm, LANES), lambda bi, i: (i, 0))
    vec = pl.BlockSpec((1, ch), lambda bi, i: (0, 0))
    hsp = pl.BlockSpec((1, 6, tm, LANES), lambda bi, i: (bi, 0, i, 0))
    head = lambda dt: jax.ShapeDtypeStruct((b, 6, s, LANES), dt)
    return pl.pallas_call(
        functools.partial(_inproj_kernel, moba_lo=moba_lo),
        grid=(b, s // tm),
        in_specs=[pl.BlockSpec((1, tm, d), lambda bi, i: (bi, i, 0)),
                  pl.BlockSpec((1, N_MOD, d), lambda bi, i: (bi, 0, 0)),
                  pl.BlockSpec((1, d), lambda bi, i: (0, 0)),
                  pl.BlockSpec((d, n), lambda bi, i: (0, 0), pipeline_mode=pl.Buffered(1)),
                  tab, tab, tab,
                  pl.BlockSpec((CONV_K, ch), lambda bi, i: (0, 0)), vec, vec, vec],
        out_specs=[pl.BlockSpec((1, tm, n_rest), lambda bi, i: (bi, i, 0)),
                   pl.BlockSpec((1, tm, ch), lambda bi, i: (bi, i, 0)), hsp, hsp,
                   pl.BlockSpec((1, 6, LANES, tm), lambda bi, i: (bi, 0, 0, i)),
                   pl.BlockSpec((1, 6, nbs, 1, LANES), lambda bi, i: (bi, 0, i, 0, 0))],
        out_shape=[jax.ShapeDtypeStruct((b, s, n_rest), F32),
                   jax.ShapeDtypeStruct((b, s, ch), BF16),
                   head(F32), head(BF16),
                   jax.ShapeDtypeStruct((b, 6, LANES, s), BF16),
                   jax.ShapeDtypeStruct((b, 6, nb, 1, LANES), F32)],
        scratch_shapes=[pltpu.VMEM((CONV_HALO, ch), F32),
                        pltpu.VMEM((CONV_HALO + tm + SUBLANES, ch), F32),
                        pltpu.VMEM((SUBLANES, CONV_HALO + tm, ch), F32)],
        name="in_proj",
        compiler_params=_cparams(("arbitrary", "arbitrary")),
    )(x, mod, g.reshape(1, d), w, *tables, conv_w, conv_b.reshape(1, ch), conv_ln_g.reshape(1, ch),
      conv_ln_b.reshape(1, ch))


def _conv_tile(u_conv, first, w_ref, b_ref, lg_ref, lb_ref, halo_ref, hs_ref, sh_ref):
    ts, ch = u_conv.shape[0], u_conv.shape[1] // 2
    h_cur = u_conv[:, :ch] * jax.nn.sigmoid(u_conv[:, ch:])
    hs_ref[0:CONV_HALO, :] = jnp.where(first, 0.0, halo_ref[...])
    hs_ref[CONV_HALO:CONV_HALO + ts, :] = h_cur
    halo_ref[...] = h_cur[ts - CONV_HALO:, :]
    hs_ref[CONV_HALO + ts:CONV_HALO + ts + SUBLANES, :] = jnp.zeros((SUBLANES, ch), F32)
    span = CONV_HALO + ts
    for r in range(SUBLANES):
        sh_ref[r] = hs_ref[pl.ds(r, span), :]
    base = CONV_HALO - (CONV_K - 1)
    acc = jnp.zeros((ts, ch), F32) + b_ref[...]
    for k in range(CONV_K):
        r = (base + k) % SUBLANES
        acc = acc + sh_ref[r, pl.ds(base + k - r, ts), :] * w_ref[k:k + 1, :]
    mu = jnp.mean(acc, axis=-1, keepdims=True)
    cen = acc - mu
    var = jnp.mean(cen * cen, axis=-1, keepdims=True)
    return _silu(cen * lax.rsqrt(var + EPS) * lg_ref[...] + lb_ref[...])


def _rope_tables(s):
    half = ROPE_DIM // 2
    inv = jnp.exp(-math.log(ROPE_THETA) * jnp.arange(0, ROPE_DIM, 2, dtype=F32) / ROPE_DIM)
    ang = jnp.arange(s, dtype=F32)[:, None] * inv[None, :]
    cs = jnp.concatenate([jnp.cos(ang), jnp.sin(ang), jnp.ones((s, 1), F32)], axis=-1)
    d = np.arange(LANES) % HEAD_DIM
    rows = 2 * half + 1
    mc, msa, msb = (np.zeros((rows, LANES), np.float32) for _ in range(3))
    for lane_i, di in enumerate(d):
        if di < ROPE_DIM:
            mc[di % half, lane_i] = 1.0
        else:
            mc[2 * half, lane_i] = 1.0
        if di < half:
            msa[half + di, lane_i] = -1.0
        elif di < ROPE_DIM:
            msb[half + (di - half), lane_i] = 1.0
    spread = lambda m: jnp.dot(cs, jnp.asarray(m), precision=lax.Precision.HIGHEST)
    return spread(mc), spread(msa), spread(msb)


def _mask_base(hh):
    return (1 - hh) * HEAD_DIM


def _moba_prep_pair(q, k, v, c, sa, sb, first_blk, p, qo_ref, ko_ref, vo_ref, km_ref):
    half = ROPE_DIM // 2

    def rope(x):
        return x * c + pltpu.roll(x, LANES - half, 1) * sa + pltpu.roll(x, half, 1) * sb

    qr, kr = rope(q), rope(k)
    lane = _iota(qr.shape, 1)
    nbs = qr.shape[0] // MOBA_BLOCK
    blk_of_row = first_blk + (_iota(qr.shape, 0) >> (MOBA_BLOCK.bit_length() - 1))
    for hh in range(2):
        own = (lane >= hh * HEAD_DIM) & (lane < (hh + 1) * HEAD_DIM)
        mb = _mask_base(hh)
        kz = jnp.where(own, kr, 0.0)
        qo_ref[0, 2 * p + hh] = jnp.where(own, qr, 0.0)
        ko_ref[0, 2 * p + hh] = jnp.where(own, kr, jnp.where(lane == mb + blk_of_row, 1.0, 0.0)).astype(BF16)
        vo_ref[0, 2 * p + hh] = jnp.where(own, v, jnp.where(lane == mb, 1.0, 0.0)).T.astype(BF16)
        for j in range(nbs):
            km_ref[0, 2 * p + hh, j] = jnp.mean(kz[j * MOBA_BLOCK:(j + 1) * MOBA_BLOCK], axis=0, keepdims=True)


def _block_mean_rows(km):
    b, h, nb = km.shape[:3]
    km = km[:, :, :, 0, :].reshape(b, h // 2, 2, nb, LANES)
    rows = [jnp.zeros((b, h // 2, LANES, LANES), F32).at[:, :, _mask_base(hh):_mask_base(hh) + nb, :].set(km[:, :, hh])
            for hh in range(2)]
    return jnp.stack(rows, axis=2).reshape(b, h, LANES, LANES)


def _moba_kernel(q_ref, ka_ref, vat_ref, kmt_ref, o_ref, sa_ref, sb_ref, m_ref, acc_ref, *, tq):
    i = pl.program_id(2)
    blk_shift = MOBA_BLOCK.bit_length() - 1
    gk = MOBA_GROUP * MOBA_BLOCK
    first_blk = i * (tq // MOBA_BLOCK)
    lane = _iota((tq, LANES), 1)
    nsel = LANES // 4
    blk_id = _iota((nsel, tq), 0)
    blk_f = blk_id.astype(F32)
    own_blk = first_blk + (_iota((nsel, tq), 1) >> blk_shift)
    neg_inf = -jnp.inf

    qas = []
    for hh in range(2):
        mb = _mask_base(hh)
        qp = q_ref[0, hh]
        gate = _dot_nt_split3(kmt_ref[0, hh, mb:mb + nsel, :], qp)
        s = jnp.where(blk_id < own_blk, gate, neg_inf)
        sel = jnp.where(blk_id == own_blk, 1.0, 0.0)
        for _ in range(MOBA_TOPK):
            m = jnp.max(s, axis=0, keepdims=True)
            idx = jnp.min(jnp.where(s == m, blk_f, float(LANES)), axis=0, keepdims=True)
            hit = blk_f == idx
            sel = jnp.where(hit, 1.0, sel)
            s = jnp.where(hit, neg_inf, s)
        neg_t = jnp.where(sel > 0.5, 0.0, MASK_NEG)
        pad_lo = jnp.zeros((mb, tq), F32)
        pad_hi = jnp.zeros((LANES - mb - nsel, tq), F32)
        negm_t = jnp.concatenate([t for t in (pad_lo, neg_t, pad_hi) if t.shape[0]], axis=0)
        qas.append((qp.T * (HEAD_DIM ** -0.5 * math.log2(math.e)) + negm_t).astype(BF16))

    def scores(buf, g):
        start = pl.multiple_of(g * gk, gk)
        for hh in range(2):
            buf[hh] = _dot(ka_ref[0, hh, pl.ds(start, gk), :], qas[hh])

    def consume(buf, g, causal):
        start = pl.multiple_of(g * gk, gk)
        for hh in range(2):
            s = buf[hh]
            if causal:
                s = jnp.where(start + _iota((gk, tq), 0) <= i * tq + _iota((gk, tq), 1), s, MASK_NEG)
            m = m_ref[hh]
            mn = jnp.maximum(m, jnp.max(s, axis=0, keepdims=True))
            p = jnp.exp2(s - mn).astype(BF16)
            pv = _dot(vat_ref[0, hh, :, pl.ds(start, gk)], p)
            m_ref[hh] = mn
            acc_ref[hh] = jnp.exp2(m - mn) * acc_ref[hh] + pv

    m_ref[...] = jnp.full(m_ref.shape, neg_inf, F32)
    acc_ref[...] = jnp.zeros(acc_ref.shape, F32)
    n_past = first_blk >> (MOBA_GROUP.bit_length() - 1)
    scores(sa_ref, 0)

    def body(j, carry):
        g = 2 * j
        scores(sb_ref, g + 1)
        consume(sa_ref, g, False)
        scores(sa_ref, g + 2)
        consume(sb_ref, g + 1, False)
        return carry

    lax.fori_loop(0, n_past >> 1, body, 0)
    odd = (n_past & 1) == 1

    @pl.when(odd)
    def _():
        scores(sb_ref, n_past)
        consume(sa_ref, n_past - 1, False)
        consume(sb_ref, n_past, True)

    @pl.when(jnp.logical_not(odd))
    def _():
        consume(sa_ref, n_past, True)

    l0, l1 = _mask_base(0), _mask_base(1)
    acc0, acc1 = acc_ref[0], acc_ref[1]
    o0 = (acc0 / acc0[l0:l0 + 1, :]).T
    o1 = (acc1 / acc1[l1:l1 + 1, :]).T
    o_ref[0, 0] = jnp.where(lane < HEAD_DIM, o0, o1).astype(o_ref.dtype)


def _moba_attention(qp, ka, vat, kmt, *, tq):
    b, _, s, _ = qp.shape
    assert s % (MOBA_GROUP * MOBA_BLOCK) == 0 and (MOBA_GROUP * MOBA_BLOCK) % tq == 0 and tq % MOBA_BLOCK == 0
    assert s // MOBA_BLOCK <= LANES // 4
    blk = tq
    gk = MOBA_GROUP * MOBA_BLOCK
    return pl.pallas_call(
        functools.partial(_moba_kernel, tq=tq),
        grid=(b, 3, s // tq),
        in_specs=[pl.BlockSpec((1, 2, blk, LANES), lambda bi, p, i: (bi, p, i, 0)),
                  pl.BlockSpec((1, 2, s, LANES), lambda bi, p, i: (bi, p, 0, 0)),
                  pl.BlockSpec((1, 2, LANES, s), lambda bi, p, i: (bi, p, 0, 0)),
                  pl.BlockSpec((1, 2, LANES, LANES), lambda bi, p, i: (bi, p, 0, 0))],
        out_specs=pl.BlockSpec((1, 1, blk, LANES), lambda bi, p, i: (bi, p, i, 0)),
        out_shape=jax.ShapeDtypeStruct((b, 3, s, LANES), BF16),
        scratch_shapes=[pltpu.VMEM((2, gk, tq), F32), pltpu.VMEM((2, gk, tq), F32),
                        pltpu.VMEM((2, 1, tq), F32), pltpu.VMEM((2, LANES, tq), F32)],
        name="moba_attention",
        compiler_params=_cparams(("arbitrary", "arbitrary", "arbitrary")),
    )(qp, ka, vat, kmt)


def _gdn_chunks_kernel(q_ref, k_ref, v_ref, hq_ref, hk_ref, hv_ref, ab_ref, wq_ref, wk_ref, wv_ref, al_ref, dt_ref,
                       x_ref, qk_ref, kdt_ref, qg_ref, eg_ref, xs_ref, *, ts):
    i = pl.program_id(1)
    p = pl.program_id(2)
    base = GDN_HALO - (GDN_CONV_K - 1)

    def conv_silu(x_ref, h_ref, w_ref):
        xs_ref[0:GDN_HALO, :] = jnp.where(i > 0, h_ref[0], 0.0)
        xs_ref[GDN_HALO:GDN_HALO + ts, :] = x_ref[0]
        acc = jnp.zeros((ts, LANES), F32)
        for k in range(GDN_CONV_K):
            acc = acc + xs_ref[pl.ds(base + k, ts), :] * w_ref[k:k + 1, :]
        return _silu(acc)

    q, k, v = conv_silu(q_ref, hq_ref, wq_ref), conv_silu(k_ref, hk_ref, wk_ref), conv_silu(v_ref, hv_ref, wv_ref)

    r2 = _iota((2 * LANES, 2 * LANES), 0)
    c2 = _iota((2 * LANES, 2 * LANES), 1)
    head_sum = jnp.where(_div64(r2) == _div64(c2), 1.0, 0.0).astype(BF16)
    ssq = _dot_data_01(jnp.concatenate([q * q, k * k], axis=1), head_sum)
    qn = q * lax.rsqrt(ssq[:, :LANES] + EPS) * (HEAD_DIM ** -0.5)
    kn = k * lax.rsqrt(ssq[:, LANES:] + EPS)

    ab = ab_ref[0]
    lane = _iota(ab.shape, 1)
    nh = 6
    xa = ab + dt_ref[...]
    softplus = jnp.maximum(xa, 0.0) + jnp.log1p(jnp.exp(-jnp.abs(xa)))
    g_beta = jnp.where(lane < nh, -jnp.exp(al_ref[...]) * softplus,
                       jnp.where(lane < 2 * nh, jax.nn.sigmoid(ab), 0.0))
    pr = _iota((LANES, 2 * LANES), 0)
    pc = _iota((LANES, 2 * LANES), 1)
    pick = jnp.where(pr == 2 * p + _div64(pc & (LANES - 1)) + jnp.where(pc < LANES, 0, nh), 1.0, 0.0).astype(BF16)
    gbx = _dot_data_01(g_beta, pick)
    gx, bx = gbx[:, :LANES], gbx[:, LANES:]

    sub = min(ts, 2 * LANES)
    tr = _iota((sub, sub), 0)
    tc = _iota((sub, sub), 1)
    in_chunk_tril = jnp.where((_div64(tr) == _div64(tc)) & (tc <= tr), 1.0, 0.0).astype(BF16)
    gcx = jnp.concatenate([_dot_01_data(in_chunk_tril, gx[r0:r0 + sub]) for r0 in range(0, ts, sub)],
                          axis=0)
    nck = ts // GDN_CHUNK
    last = gcx.reshape(nck, GDN_CHUNK, LANES)[:, GDN_CHUNK - 1:GDN_CHUNK, :]
    glx = jnp.broadcast_to(last, (nck, GDN_CHUNK, LANES)).reshape(ts, LANES)
    eg = jnp.exp(gcx)

    kb = kn * bx
    qg_ref[0, 0] = (qn * eg).astype(BF16)
    eg_ref[0, 0] = jnp.exp(glx)
    _gdn_solve_tile(qn.astype(BF16), kn.astype(BF16), kb.astype(BF16), v * bx, kb * eg, kn * jnp.exp(glx - gcx), gcx,
                    x_ref, qk_ref, kdt_ref)


def _gdn_chunks(u, conv_w, a_log, dt_bias, *, q_lane_block, ab_lane_block, ts):
    b, s, _ = u.shape
    qb = q_lane_block
    hb = ts // GDN_HALO
    nh = a_log.shape[0]
    pad = lambda t: jnp.zeros((1, LANES), F32).at[0, :nh].set(t)
    cur = lambda off: pl.BlockSpec((1, ts, LANES), lambda bi, i, p: (bi, i, qb + off + p))
    halo = lambda off: pl.BlockSpec((1, GDN_HALO, LANES), lambda bi, i, p: (bi, jnp.maximum(i * hb - 1, 0), qb + off + p))
    wsp = lambda off: pl.BlockSpec((GDN_CONV_K, LANES), lambda bi, i, p: (0, off + p))
    vec = pl.BlockSpec((1, LANES), lambda bi, i, p: (0, 0))
    one = pl.BlockSpec((1, 1, ts, LANES), lambda bi, i, p: (bi, p, i, 0))
    two = pl.BlockSpec((1, 1, 2 * ts, LANES), lambda bi, i, p: (bi, p, i, 0))
    sh1 = lambda dt: jax.ShapeDtypeStruct((b, 3, s, LANES), dt)
    sh2 = lambda dt: jax.ShapeDtypeStruct((b, 3, 2 * s, LANES), dt)
    return pl.pallas_call(
        functools.partial(_gdn_chunks_kernel, ts=ts),
        grid=(b, s // ts, 3),
        in_specs=[cur(0), cur(3), cur(6), halo(0), halo(3), halo(6),
                  pl.BlockSpec((1, ts, LANES), lambda bi, i, p: (bi, i, ab_lane_block)),
                  wsp(0), wsp(3), wsp(6), vec, vec],
        out_specs=[two, two, two, one, one],
        out_shape=[sh2(F32), sh2(BF16), sh2(BF16), sh1(BF16), sh1(F32)],
        scratch_shapes=[pltpu.VMEM((GDN_HALO + ts, LANES), F32)],
        name="gdn_chunks",
        compiler_params=_cparams(("arbitrary", "arbitrary", "arbitrary")),
    )(u, u, u, u, u, u, u, conv_w, conv_w, conv_w, pad(a_log), pad(dt_bias))


def _stack2(top, bot):
    return jnp.concatenate([top, bot], axis=0)


def _gdn_solve_tile(qn_t, kn_t, kb_t, vb_t, kbg_t, kd_t, gc_t, x_ref, qk_ref, kdt_ref):
    cs = GDN_CHUNK
    n2 = 2 * cs
    chunks = qn_t.shape[0] // cs
    lane = _iota((cs, LANES), 1)
    low = lane < HEAD_DIM
    r = _iota((n2, n2), 0)
    c = _iota((n2, n2), 1)
    bd = _div64(r) == _div64(c)
    eye = jnp.where(r == c, 1.0, 0.0)
    diag8 = (r >> 3) == (c >> 3)
    merge_masks = [((r >> (sh + 1)) == (c >> (sh + 1))) & (((r >> sh) & 1) == 1) & (((c >> sh) & 1) == 0)
                   for sh in (3, 4, 5)]

    def per_head_rows(t, fill=0.0):
        return _stack2(jnp.where(low, t, fill), jnp.where(low, fill, t))

    cis = range(chunks)
    rows = [slice(ci * cs, (ci + 1) * cs) for ci in cis]
    rows2 = [pl.ds(ci * n2, n2) for ci in cis]
    kn = [per_head_rows(kn_t[rw]) for rw in rows]
    kb = [per_head_rows(kb_t[rw]) for rw in rows]
    qn = [per_head_rows(qn_t[rw]) for rw in rows]
    kk = [_dot_nt(kb[ci], kn[ci]) for ci in cis]
    qk = [_dot_nt(qn[ci], kn[ci]) for ci in cis]
    decay = []
    for rw in rows:
        gc = gc_t[rw]
        gsw = pltpu.roll(gc, HEAD_DIM, 1)
        gcol = _stack2(jnp.where(low, gc, gsw), jnp.where(low, gsw, gc))
        decay.append(jnp.exp(jnp.minimum(gcol - gcol.T, 0.0)))
    lmat = [jnp.where(c < r, kk[ci] * decay[ci], 0.0) for ci in cis]
    for ci in cis:
        qk_ref[0, 0, rows2[ci], :] = jnp.where(c <= r, qk[ci] * decay[ci], 0.0).astype(BF16)

    ld = [jnp.where(diag8, lm, 0.0) for lm in lmat]
    ld2 = [_dot_split2(m, m) for m in ld]
    ld4 = [_dot_split2(m, m) for m in ld2]
    t = [eye - m for m in ld]
    t = [t[ci] + _dot_split2(t[ci], ld2[ci]) for ci in cis]
    t = [t[ci] + _dot_split2(t[ci], ld4[ci]) for ci in cis]
    for lower_left in merge_masks:
        tb = [m.astype(BF16) for m in t]
        inner = [_dot(jnp.where(lower_left, lmat[ci], 0.0).astype(BF16), tb[ci]) for ci in cis]
        t = [t[ci] - _dot(tb[ci], inner[ci].astype(BF16)) for ci in cis]
    for ci in cis:
        kbg = kbg_t[rows[ci]]
        vsw = pltpu.roll(vb_t[rows[ci]], HEAD_DIM, 1)
        x = jnp.where(bd, _stack2(kbg, kbg), _stack2(vsw, vsw))
        x_ref[0, 0, rows2[ci], :] = _dot(t[ci].astype(BF16), x.astype(BF16))
    for ci in cis:
        kd = kd_t[rows[ci]]
        kdt_ref[0, 0, rows2[ci], :] = jnp.where(bd, _stack2(kd, kd), 0.0).T.astype(BF16)


def _gdn_scan_kernel(x_ref, qk_ref, kdt_ref, qg_ref, eg_ref, z0_ref, z1_ref, z2_ref, ng_ref, o_ref, st_ref, raw_ref,
                     *, chunks, batch):
    cs = GDN_CHUNK
    n2 = 2 * cs
    r = _iota((n2, n2), 0)
    c = _iota((n2, n2), 1)
    bd = _div64(r) == _div64(c)
    hr = _iota((LANES, LANES), 0)
    hc = _iota((LANES, LANES), 1)
    head_mean = jnp.where(_div64(hr) == _div64(hc), 1.0 / HEAD_DIM, 0.0).astype(BF16)
    z_refs = (z0_ref, z1_ref, z2_ref)

    @pl.when(pl.program_id(0) == 0)
    def _():
        st_ref[...] = jnp.zeros_like(st_ref)

    def body(ci, carry):
        start = pl.multiple_of(ci * cs, cs)
        rows = pl.ds(start, cs)
        rows2 = pl.ds(pl.multiple_of(ci * n2, n2), n2)
        chains = [(bi, p) for bi in range(batch) for p in range(3)]
        ks = range(len(chains))
        st = [st_ref[bi * 3 + p] for bi, p in chains]
        stb = [t.astype(BF16) for t in st]
        x = [x_ref[bi, p, rows2, :] for bi, p in chains]
        ws = [_dot(jnp.where(bd, x[k], 0.0).astype(BF16), stb[k]) for k in ks]
        vnb = [(jnp.where(bd, 0.0, x[k]) - ws[k]).astype(BF16) for k in ks]
        qbd = [jnp.where(bd, _stack2(qg, qg), jnp.zeros((), BF16)) for qg in (qg_ref[bi, p, rows, :] for bi, p in chains)]
        o = [_dot(qbd[k], stb[k]) + _dot(qk_ref[bi, p, rows2, :], vnb[k]) for k, (bi, p) in enumerate(chains)]
        upd = [_dot(kdt_ref[bi, p, rows2, :], vnb[k]) for k, (bi, p) in enumerate(chains)]
        for k, (bi, p) in enumerate(chains):
            eg = pltpu.roll(eg_ref[bi, p, pl.ds(start, 8), :], HEAD_DIM, 1)[0:1, :]
            st_ref[bi * 3 + p] = st[k] * eg + upd[k]
        for k, (bi, p) in enumerate(chains):
            t = o[k]
            raw_ref[bi, p, rows, :] = pltpu.roll(t[:cs] + t[cs:], HEAD_DIM, 1)
        return carry

    lax.fori_loop(0, chunks, body, 0)

    for bi in range(batch):
        for p in range(3):
            op = raw_ref[bi, p]
            ms = _dot_data_01(op * op, head_mean)
            o_ref[bi, p] = (op * lax.rsqrt(ms + EPS) * ng_ref[...] * _silu(z_refs[p][bi])).astype(o_ref.dtype)


def _gdn_scan(x, qk, kdt, qg, eg, u, norm_g, *, z_lane_block, ts):
    b, _, s, _ = qg.shape
    chunks = ts // GDN_CHUNK
    zb = z_lane_block
    two = pl.BlockSpec((b, 3, 2 * ts, LANES), lambda i: (0, 0, i, 0))
    one = pl.BlockSpec((b, 3, ts, LANES), lambda i: (0, 0, i, 0))
    zsp = lambda p: pl.BlockSpec((b, ts, LANES), lambda i: (0, i, zb + p))
    ng = jnp.concatenate([norm_g, norm_g]).reshape(1, LANES)
    return pl.pallas_call(
        functools.partial(_gdn_scan_kernel, chunks=chunks, batch=b),
        grid=(s // ts,),
        in_specs=[two, two, two, one, one, zsp(0), zsp(1), zsp(2), pl.BlockSpec((1, LANES), lambda i: (0, 0))],
        out_specs=one,
        out_shape=jax.ShapeDtypeStruct((b, 3, s, LANES), BF16),
        scratch_shapes=[pltpu.VMEM((b * 3, LANES, LANES), F32),
                        pltpu.VMEM((b, 3, ts, LANES), F32)],
        name="gdn_scan",
        compiler_params=_cparams(("arbitrary",)),
    )(x, qk, kdt, qg, eg, u, u, u, ng)


def _outproj_kernel(x_ref, mod_ref, yc_ref, ym_ref, yg_ref, w_ref, o_ref):
    parts = [yc_ref[0]] + [y_ref[0, p] for y_ref in (ym_ref, yg_ref) for p in range(3)]
    y = jnp.concatenate([t.astype(BF16) for t in parts], axis=-1)
    o_ref[0] = x_ref[0] + mod_ref[0][5:6] * _dot(y, w_ref[...])


def _outproj(x, mod, yc, ym, yg, w, *, tm):
    b, s, d = x.shape
    ch = yc.shape[-1]
    return pl.pallas_call(
        _outproj_kernel,
        grid=(b, s // tm),
        in_specs=[pl.BlockSpec((1, tm, d), lambda bi, i: (bi, i, 0)),
                  pl.BlockSpec((1, N_MOD, d), lambda bi, i: (bi, 0, 0)),
                  pl.BlockSpec((1, tm, ch), lambda bi, i: (bi, i, 0)),
                  pl.BlockSpec((1, 3, tm, LANES), lambda bi, i: (bi, 0, i, 0)),
                  pl.BlockSpec((1, 3, tm, LANES), lambda bi, i: (bi, 0, i, 0)),
                  pl.BlockSpec(w.shape, lambda bi, i: (0, 0), pipeline_mode=pl.Buffered(1))],
        out_specs=pl.BlockSpec((1, tm, d), lambda bi, i: (bi, i, 0)),
        out_shape=jax.ShapeDtypeStruct((b, s, d), F32),
        name="out_proj",
        compiler_params=_cparams(("arbitrary", "arbitrary")),
    )(x, mod, yc, ym, yg, w)


def _token_mixing(x, mod, ln_g, w_in, conv_w, conv_b, conv_ln_g, conv_ln_b, gdn_conv_w, gdn_a_log, gdn_dt_bias,
                  gdn_norm_g, w_out, tables, *, tm, tq_moba, ts_gdn, ts_scan):
    d, d_in = w_in.shape
    n_pad = -(-d_in // LANES) * LANES
    w_in_p = jnp.zeros((d, n_pad), BF16).at[:, :d_in].set(w_in.astype(BF16))
    u, y_conv, qp, ka, va, km = _inproj(x, mod, ln_g, w_in_p, tables, conv_w, conv_b, conv_ln_g, conv_ln_b, tm=tm)

    q_blk_gdn = 0
    z_blk = q_blk_gdn + 9
    ab_blk = z_blk + 3

    y_moba = _moba_attention(qp, ka, va, _block_mean_rows(km), tq=tq_moba)

    xs, qk, kdt, qg, eg = _gdn_chunks(u, gdn_conv_w, gdn_a_log, gdn_dt_bias, q_lane_block=q_blk_gdn,
                                       ab_lane_block=ab_blk, ts=ts_gdn)
    y_gdn = _gdn_scan(xs, qk, kdt, qg, eg, u, gdn_norm_g, z_lane_block=z_blk, ts=ts_scan)

    return _outproj(x, mod, y_conv, y_moba, y_gdn, w_out.astype(BF16), tm=tm)


def kernel(x, c, w_ada, b_ada, ln_ffn1_g, ffn1_w_gate, ffn1_w_up, ffn1_w_down, ln_mix_g, w_in, conv_w, conv_b, conv_ln_g, conv_ln_b, gdn_conv_w, gdn_a_log, gdn_dt_bias, gdn_norm_g, w_out, ln_ffn2_g, ffn2_w_gate, ffn2_w_up, ffn2_w_down, final_g):
    depth = w_ada.shape[0]
    s = x.shape[1]
    tm = min(512, s)
    tiles = dict(tm=tm, tq_moba=min(512, s), ts_gdn=min(1024, s),
                 ts_scan=min(512, s))
    mods = _ada_mod(c, w_ada, b_ada)
    tables = _rope_tables(s)
    for l in range(depth):
        mod = mods[l]
        x = _ffn(x, mod, ln_ffn1_g, ffn1_w_gate, ffn1_w_up, ffn1_w_down, final_g,
                 layer=l, mod_base=0, final=False, tm=tm)
        x = _token_mixing(x, mod, ln_mix_g[l], w_in[l], conv_w[l], conv_b[l], conv_ln_g[l], conv_ln_b[l],
                          gdn_conv_w[l], gdn_a_log[l], gdn_dt_bias[l], gdn_norm_g[l], w_out[l], tables, **tiles)
        x = _ffn(x, mod, ln_ffn2_g, ffn2_w_gate, ffn2_w_up, ffn2_w_down, final_g,
                 layer=l, mod_base=6, final=(l == depth - 1), tm=tm)
    return x
```

```python
import functools
import math

import numpy as np
import jax
import jax.numpy as jnp
from jax import lax
from jax.experimental import pallas as pl
from jax.experimental.pallas import tpu as pltpu

F32 = jnp.float32
BF16 = jnp.bfloat16

EPS = 1e-6
HEAD_DIM = 64
LANES = 128
SUBLANES = 8
N_MOD = 9
FFN_CHUNK = 256
CONV_K = 31
CONV_HALO = 32
MOBA_BLOCK = 256
MOBA_TOPK = 3
MOBA_GROUP = 4
ROPE_DIM = HEAD_DIM // 4
ROPE_THETA = 500000.0
GDN_CONV_K = 4
GDN_HALO = 8
GDN_CHUNK = 64
MASK_NEG = -(2.0 ** 100)
VMEM_LIMIT = 56 * 1024 * 1024


def _cparams(sem):
    return pltpu.CompilerParams(dimension_semantics=sem, vmem_limit_bytes=VMEM_LIMIT)


def _iota(shape, dim):
    return lax.broadcasted_iota(jnp.int32, shape, dim)


def _div64(t):
    return t >> 6


def _dot(a, b):
    return jnp.dot(a, b, preferred_element_type=F32)


def _dot_nt(a, b):
    return lax.dot_general(a, b, (((1,), (1,)), ((), ())), preferred_element_type=F32)


def _split3(x):
    x0 = x.astype(BF16)
    r1 = x - x0.astype(F32)
    x1 = r1.astype(BF16)
    x2 = (r1 - x1.astype(F32)).astype(BF16)
    return x0, x1, x2


def _dot_nt_split3(a, b):
    a0, a1, a2 = _split3(a)
    b0, b1, b2 = _split3(b)
    small = _dot_nt(a0, b2) + _dot_nt(a1, b1) + _dot_nt(a2, b0)
    mid = _dot_nt(a0, b1) + _dot_nt(a1, b0)
    return _dot_nt(a0, b0) + (mid + small)


def _dot_data_01(x, m01):
    x0, x1, x2 = _split3(x)
    return _dot(x0, m01) + (_dot(x1, m01) + _dot(x2, m01))


def _dot_01_data(m01, x):
    x0, x1, x2 = _split3(x)
    return _dot(m01, x0) + (_dot(m01, x1) + _dot(m01, x2))


def _dot_split2(a, b):
    a0 = a.astype(BF16)
    a1 = (a - a0.astype(F32)).astype(BF16)
    b0 = b.astype(BF16)
    b1 = (b - b0.astype(F32)).astype(BF16)
    return _dot(a0, b0) + (_dot(a0, b1) + _dot(a1, b0))


def _silu(x):
    return x * jax.nn.sigmoid(x)


def _modulate(x, g, shift, scale):
    ms = jnp.mean(x * x, axis=-1, keepdims=True)
    y = x * lax.rsqrt(ms + EPS) * g
    return y * (1.0 + scale) + shift


def _ada_kernel(c_ref, w_ref, b_ref, o_ref):
    ca = _silu(c_ref[...]).astype(BF16)
    o_ref[0] = _dot(ca, w_ref[0].astype(BF16)) + b_ref[0]


def _ada_mod(c, w_ada, b_ada):
    depth, d, nd = w_ada.shape
    b = c.shape[0]
    rows = 8
    c8 = jnp.zeros((rows, d), F32).at[:b].set(c)
    out = pl.pallas_call(
        _ada_kernel,
        grid=(depth, nd // d),
        in_specs=[pl.BlockSpec((rows, d), lambda l, j: (0, 0)),
                  pl.BlockSpec((1, d, d), lambda l, j: (l, 0, j)),
                  pl.BlockSpec((1, 1, d), lambda l, j: (l, 0, j))],
        out_specs=pl.BlockSpec((1, rows, d), lambda l, j: (l, 0, j)),
        out_shape=jax.ShapeDtypeStruct((depth, rows, nd), F32),
        name="ada_mod",
        compiler_params=_cparams(("arbitrary", "arbitrary")),
    )(c8, w_ada, b_ada.reshape(depth, 1, nd))
    return out[:, :b].reshape(depth, b, nd // d, d)


def _ffn_kernel(x_ref, mod_ref, g_ref, wg_ref, wu_ref, wd_ref, fg_ref, o_ref, *, mod_base, final):
    x = x_ref[0]
    m = mod_ref[0]
    shift, scale, gate = m[mod_base:mod_base + 1], m[mod_base + 1:mod_base + 2], m[mod_base + 2:mod_base + 3]
    h = _modulate(x, g_ref[0], shift, scale).astype(BF16)
    down = None
    for c0 in range(0, wg_ref.shape[2], FFN_CHUNK):
        a = _dot(h, wg_ref[0, :, c0:c0 + FFN_CHUNK].astype(BF16))
        u = _dot(h, wu_ref[0, :, c0:c0 + FFN_CHUNK].astype(BF16))
        act = (_silu(a) * u).astype(BF16)
        part = _dot(act, wd_ref[0, c0:c0 + FFN_CHUNK, :].astype(BF16))
        down = part if down is None else down + part
    y = x + (0.5 * gate) * down
    if final:
        y = y * lax.rsqrt(jnp.mean(y * y, axis=-1, keepdims=True) + EPS) * fg_ref[...]
    o_ref[0] = y


def _ffn(x, mod, g, wg, wu, wd, final_g, *, layer, mod_base, final, tm):
    b, s, d = x.shape
    depth, _, f = wg.shape
    resident = dict(pipeline_mode=pl.Buffered(1))
    return pl.pallas_call(
        functools.partial(_ffn_kernel, mod_base=mod_base, final=final),
        grid=(b, s // tm),
        in_specs=[pl.BlockSpec((1, tm, d), lambda bi, i: (bi, i, 0)),
                  pl.BlockSpec((1, N_MOD, d), lambda bi, i: (bi, 0, 0)),
                  pl.BlockSpec((1, 1, d), lambda bi, i: (layer, 0, 0)),
                  pl.BlockSpec((1, d, f), lambda bi, i: (layer, 0, 0), **resident),
                  pl.BlockSpec((1, d, f), lambda bi, i: (layer, 0, 0), **resident),
                  pl.BlockSpec((1, f, d), lambda bi, i: (layer, 0, 0), **resident),
                  pl.BlockSpec((1, d), lambda bi, i: (0, 0))],
        out_specs=pl.BlockSpec((1, tm, d), lambda bi, i: (bi, i, 0)),
        out_shape=jax.ShapeDtypeStruct((b, s, d), F32),
        name="ffn",
        compiler_params=_cparams(("arbitrary", "arbitrary")),
    )(x, mod, g.reshape(depth, 1, d), wg, wu, wd, final_g.reshape(1, d))


def _inproj_kernel(x_ref, mod_ref, g_ref, w_ref, c_ref, sa_ref, sb_ref, cw_ref, cb_ref, clg_ref, clb_ref,
                   o_ref, yc_ref, qo_ref, ko_ref, vo_ref, km_ref, halo_ref, hs_ref, sh_ref, *, moba_lo):
    i = pl.program_id(1)
    m = mod_ref[0]
    h = _modulate(x_ref[0], g_ref[...], m[3:4], m[4:5]).astype(BF16)
    u = _dot(h, w_ref[...])
    moba_hi = moba_lo + 9 * LANES
    o_ref[0] = u[:, moba_hi:]
    yc_ref[0] = _conv_tile(u[:, :moba_lo], i == 0, cw_ref, cb_ref, clg_ref, clb_ref,
                           halo_ref, hs_ref, sh_ref).astype(yc_ref.dtype)
    c, sa, sb = c_ref[...], sa_ref[...], sb_ref[...]
    first_blk = i * (u.shape[0] // MOBA_BLOCK)
    for p in range(3):
        q, k, v = (u[:, moba_lo + (3 * j + p) * LANES:moba_lo + (3 * j + p + 1) * LANES] for j in range(3))
        _moba_prep_pair(q, k, v, c, sa, sb, first_blk, p, qo_ref, ko_ref, vo_ref, km_ref)


def _inproj(x, mod, g, w, tables, conv_w, conv_b, conv_ln_g, conv_ln_b, *, tm):
    b, s, d = x.shape
    n = w.shape[1]
    ch = conv_w.shape[1]
    moba_lo = 2 * ch
    n_rest = n - moba_lo - 9 * LANES
    nb = s // MOBA_BLOCK
    nbs = tm // MOBA_BLOCK
    tab = pl.BlockSpec((tm, LANES), lambda bi, i: (i, 0))
    vec = pl.BlockSpec((1, ch), lambda bi, i: (0, 0))
    hsp = pl.BlockSpec((1, 6, tm, LANES), lambda bi, i: (bi, 0, i, 0))
    head = lambda dt: jax.ShapeDtypeStruct((b, 6, s, LANES), dt)
    return pl.pallas_call(
        functools.partial(_inproj_kernel, moba_lo=moba_lo),
        grid=(b, s // tm),
        in_specs=[pl.BlockSpec((1, tm, d), lambda bi, i: (bi, i, 0)),
                  pl.BlockSpec((1, N_MOD, d), lambda bi, i: (bi, 0, 0)),
                  pl.BlockSpec((1, d), lambda bi, i: (0, 0)),
                  pl.BlockSpec((d, n), lambda bi, i: (0, 0), pipeline_mode=pl.Buffered(1)),
                  tab, tab, tab,
                  pl.BlockSpec((CONV_K, ch), lambda bi, i: (0, 0)), vec, vec, vec],
        out_specs=[pl.BlockSpec((1, tm, n_rest), lambda bi, i: (bi, i, 0)),
                   pl.BlockSpec((1, tm, ch), lambda bi, i: (bi, i, 0)), hsp, hsp,
                   pl.BlockSpec((1, 6, LANES, tm), lambda bi, i: (bi, 0, 0, i)),
                   pl.BlockSpec((1, 6, nbs, 1, LANES), lambda bi, i: (bi, 0, i, 0, 0))],
        out_shape=[jax.ShapeDtypeStruct((b, s, n_rest), F32),
                   jax.ShapeDtypeStruct((b, s, ch), BF16),
                   head(F32), head(BF16),
                   jax.ShapeDtypeStruct((b, 6, LANES, s), BF16),
                   jax.ShapeDtypeStruct((b, 6, nb, 1, LANES), F32)],
        scratch_shapes=[pltpu.VMEM((CONV_HALO, ch), F32),
                        pltpu.VMEM((CONV_HALO + tm + SUBLANES, ch), F32),
                        pltpu.VMEM((SUBLANES, CONV_HALO + tm, ch), F32)],
        name="in_proj",
        compiler_params=_cparams(("arbitrary", "arbitrary")),
    )(x, mod, g.reshape(1, d), w, *tables, conv_w, conv_b.reshape(1, ch), conv_ln_g.reshape(1, ch),
      conv_ln_b.reshape(1, ch))


def _conv_tile(u_conv, first, w_ref, b_ref, lg_ref, lb_ref, halo_ref, hs_ref, sh_ref):
    ts, ch = u_conv.shape[0], u_conv.shape[1] // 2
    h_cur = u_conv[:, :ch] * jax.nn.sigmoid(u_conv[:, ch:])
    hs_ref[0:CONV_HALO, :] = jnp.where(first, 0.0, halo_ref[...])
    hs_ref[CONV_HALO:CONV_HALO + ts, :] = h_cur
    halo_ref[...] = h_cur[ts - CONV_HALO:, :]
    hs_ref[CONV_HALO + ts:CONV_HALO + ts + SUBLANES, :] = jnp.zeros((SUBLANES, ch), F32)
    span = CONV_HALO + ts
    for r in range(SUBLANES):
        sh_ref[r] = hs_ref[pl.ds(r, span), :]
    base = CONV_HALO - (CONV_K - 1)
    acc = jnp.zeros((ts, ch), F32) + b_ref[...]
    for k in range(CONV_K):
        r = (base + k) % SUBLANES
        acc = acc + sh_ref[r, pl.ds(base + k - r, ts), :] * w_ref[k:k + 1, :]
    mu = jnp.mean(acc, axis=-1, keepdims=True)
    cen = acc - mu
    var = jnp.mean(cen * cen, axis=-1, keepdims=True)
    return _silu(cen * lax.rsqrt(var + EPS) * lg_ref[...] + lb_ref[...])


def _rope_tables(s):
    half = ROPE_DIM // 2
    inv = jnp.exp(-math.log(ROPE_THETA) * jnp.arange(0, ROPE_DIM, 2, dtype=F32) / ROPE_DIM)
    ang = jnp.arange(s, dtype=F32)[:, None] * inv[None, :]
    cs = jnp.concatenate([jnp.cos(ang), jnp.sin(ang), jnp.ones((s, 1), F32)], axis=-1)
    d = np.arange(LANES) % HEAD_DIM
    rows = 2 * half + 1
    mc, msa, msb = (np.zeros((rows, LANES), np.float32) for _ in range(3))
    for lane_i, di in enumerate(d):
        if di < ROPE_DIM:
            mc[di % half, lane_i] = 1.0
        else:
            mc[2 * half, lane_i] = 1.0
        if di < half:
            msa[half + di, lane_i] = -1.0
        elif di < ROPE_DIM:
            msb[half + (di - half), lane_i] = 1.0
    spread = lambda m: jnp.dot(cs, jnp.asarray(m), precision=lax.Precision.HIGHEST)
    return spread(mc), spread(msa), spread(msb)


def _mask_base(hh):
    return (1 - hh) * HEAD_DIM


def _moba_prep_pair(q, k, v, c, sa, sb, first_blk, p, qo_ref, ko_ref, vo_ref, km_ref):
    half = ROPE_DIM // 2

    def rope(x):
        return x * c + pltpu.roll(x, LANES - half, 1) * sa + pltpu.roll(x, half, 1) * sb

    qr, kr = rope(q), rope(k)
    lane = _iota(qr.shape, 1)
    nbs = qr.shape[0] // MOBA_BLOCK
    blk_of_row = first_blk + (_iota(qr.shape, 0) >> (MOBA_BLOCK.bit_length() - 1))
    for hh in range(2):
        own = (lane >= hh * HEAD_DIM) & (lane < (hh + 1) * HEAD_DIM)
        mb = _mask_base(hh)
        kz = jnp.where(own, kr, 0.0)
        qo_ref[0, 2 * p + hh] = jnp.where(own, qr, 0.0)
        ko_ref[0, 2 * p + hh] = jnp.where(own, kr, jnp.where(lane == mb + blk_of_row, 1.0, 0.0)).astype(BF16)
        vo_ref[0, 2 * p + hh] = jnp.where(own, v, jnp.where(lane == mb, 1.0, 0.0)).T.astype(BF16)
        for j in range(nbs):
            km_ref[0, 2 * p + hh, j] = jnp.mean(kz[j * MOBA_BLOCK:(j + 1) * MOBA_BLOCK], axis=0, keepdims=True)


def _block_mean_rows(km):
    b, h, nb = km.shape[:3]
    km = km[:, :, :, 0, :].reshape(b, h // 2, 2, nb, LANES)
    rows = [jnp.zeros((b, h // 2, LANES, LANES), F32).at[:, :, _mask_base(hh):_mask_base(hh) + nb, :].set(km[:, :, hh])
            for hh in range(2)]
    return jnp.stack(rows, axis=2).reshape(b, h, LANES, LANES)


def _moba_kernel(q_ref, ka_ref, vat_ref, kmt_ref, o_ref, sa_ref, sb_ref, m_ref, acc_ref, *, tq):
    i = pl.program_id(2)
    blk_shift = MOBA_BLOCK.bit_length() - 1
    gk = MOBA_GROUP * MOBA_BLOCK
    first_blk = i * (tq // MOBA_BLOCK)
    lane = _iota((tq, LANES), 1)
    nsel = LANES // 4
    blk_id = _iota((nsel, tq), 0)
    blk_f = blk_id.astype(F32)
    own_blk = first_blk + (_iota((nsel, tq), 1) >> blk_shift)
    neg_inf = -jnp.inf

    qas = []
    for hh in range(2):
        mb = _mask_base(hh)
        qp = q_ref[0, hh]
        gate = _dot_nt_split3(kmt_ref[0, hh, mb:mb + nsel, :], qp)
        s = jnp.where(blk_id < own_blk, gate, neg_inf)
        sel = jnp.where(blk_id == own_blk, 1.0, 0.0)
        for _ in range(MOBA_TOPK):
            m = jnp.max(s, axis=0, keepdims=True)
            idx = jnp.min(jnp.where(s == m, blk_f, float(LANES)), axis=0, keepdims=True)
            hit = blk_f == idx
            sel = jnp.where(hit, 1.0, sel)
            s = jnp.where(hit, neg_inf, s)
        neg_t = jnp.where(sel > 0.5, 0.0, MASK_NEG)
        pad_lo = jnp.zeros((mb, tq), F32)
        pad_hi = jnp.zeros((LANES - mb - nsel, tq), F32)
        negm_t = jnp.concatenate([t for t in (pad_lo, neg_t, pad_hi) if t.shape[0]], axis=0)
        qas.append((qp.T * (HEAD_DIM ** -0.5 * math.log2(math.e)) + negm_t).astype(BF16))

    def scores(buf, g):
        start = pl.multiple_of(g * gk, gk)
        for hh in range(2):
            buf[hh] = _dot(ka_ref[0, hh, pl.ds(start, gk), :], qas[hh])

    def consume(buf, g, causal):
        start = pl.multiple_of(g * gk, gk)
        for hh in range(2):
            s = buf[hh]
            if causal:
                s = jnp.where(start + _iota((gk, tq), 0) <= i * tq + _iota((gk, tq), 1), s, MASK_NEG)
            m = m_ref[hh]
            mn = jnp.maximum(m, jnp.max(s, axis=0, keepdims=True))
            p = jnp.exp2(s - mn).astype(BF16)
            pv = _dot(vat_ref[0, hh, :, pl.ds(start, gk)], p)
            m_ref[hh] = mn
            acc_ref[hh] = jnp.exp2(m - mn) * acc_ref[hh] + pv

    m_ref[...] = jnp.full(m_ref.shape, neg_inf, F32)
    acc_ref[...] = jnp.zeros(acc_ref.shape, F32)
    n_past = first_blk >> (MOBA_GROUP.bit_length() - 1)
    scores(sa_ref, 0)

    def body(j, carry):
        g = 2 * j
        scores(sb_ref, g + 1)
        consume(sa_ref, g, False)
        scores(sa_ref, g + 2)
        consume(sb_ref, g + 1, False)
        return carry

    lax.fori_loop(0, n_past >> 1, body, 0)
    odd = (n_past & 1) == 1

    @pl.when(odd)
    def _():
        scores(sb_ref, n_past)
        consume(sa_ref, n_past - 1, False)
        consume(sb_ref, n_past, True)

    @pl.when(jnp.logical_not(odd))
    def _():
        consume(sa_ref, n_past, True)

    l0, l1 = _mask_base(0), _mask_base(1)
    acc0, acc1 = acc_ref[0], acc_ref[1]
    o0 = (acc0 / acc0[l0:l0 + 1, :]).T
    o1 = (acc1 / acc1[l1:l1 + 1, :]).T
    o_ref[0, 0] = jnp.where(lane < HEAD_DIM, o0, o1).astype(o_ref.dtype)


def _moba_attention(qp, ka, vat, kmt, *, tq):
    b, _, s, _ = qp.shape
    assert s % (MOBA_GROUP * MOBA_BLOCK) == 0 and (MOBA_GROUP * MOBA_BLOCK) % tq == 0 and tq % MOBA_BLOCK == 0
    assert s // MOBA_BLOCK <= LANES // 4
    blk = tq
    gk = MOBA_GROUP * MOBA_BLOCK
    return pl.pallas_call(
        functools.partial(_moba_kernel, tq=tq),
        grid=(b, 3, s // tq),
        in_specs=[pl.BlockSpec((1, 2, blk, LANES), lambda bi, p, i: (bi, p, i, 0)),
                  pl.BlockSpec((1, 2, s, LANES), lambda bi, p, i: (bi, p, 0, 0)),
                  pl.BlockSpec((1, 2, LANES, s), lambda bi, p, i: (bi, p, 0, 0)),
                  pl.BlockSpec((1, 2, LANES, LANES), lambda bi, p, i: (bi, p, 0, 0))],
        out_specs=pl.BlockSpec((1, 1, blk, LANES), lambda bi, p, i: (bi, p, i, 0)),
        out_shape=jax.ShapeDtypeStruct((b, 3, s, LANES), BF16),
        scratch_shapes=[pltpu.VMEM((2, gk, tq), F32), pltpu.VMEM((2, gk, tq), F32),
                        pltpu.VMEM((2, 1, tq), F32), pltpu.VMEM((2, LANES, tq), F32)],
        name="moba_attention",
        compiler_params=_cparams(("arbitrary", "arbitrary", "arbitrary")),
    )(qp, ka, vat, kmt)


def _gdn_chunks_kernel(q_ref, k_ref, v_ref, hq_ref, hk_ref, hv_ref, ab_ref, wq_ref, wk_ref, wv_ref, al_ref, dt_ref,
                       x_ref, qk_ref, kdt_ref, qg_ref, eg_ref, xs_ref, *, ts):
    i = pl.program_id(1)
    p = pl.program_id(2)
    base = GDN_HALO - (GDN_CONV_K - 1)

    def conv_silu(x_ref, h_ref, w_ref):
        xs_ref[0:GDN_HALO, :] = jnp.where(i > 0, h_ref[0], 0.0)
        xs_ref[GDN_HALO:GDN_HALO + ts, :] = x_ref[0]
        acc = jnp.zeros((ts, LANES), F32)
        for k in range(GDN_CONV_K):
            acc = acc + xs_ref[pl.ds(base + k, ts), :] * w_ref[k:k + 1, :]
        return _silu(acc)

    q, k, v = conv_silu(q_ref, hq_ref, wq_ref), conv_silu(k_ref, hk_ref, wk_ref), conv_silu(v_ref, hv_ref, wv_ref)

    r2 = _iota((2 * LANES, 2 * LANES), 0)
    c2 = _iota((2 * LANES, 2 * LANES), 1)
    head_sum = jnp.where(_div64(r2) == _div64(c2), 1.0, 0.0).astype(BF16)
    ssq = _dot_data_01(jnp.concatenate([q * q, k * k], axis=1), head_sum)
    qn = q * lax.rsqrt(ssq[:, :LANES] + EPS) * (HEAD_DIM ** -0.5)
    kn = k * lax.rsqrt(ssq[:, LANES:] + EPS)

    ab = ab_ref[0]
    lane = _iota(ab.shape, 1)
    nh = 6
    xa = ab + dt_ref[...]
    softplus = jnp.maximum(xa, 0.0) + jnp.log1p(jnp.exp(-jnp.abs(xa)))
    g_beta = jnp.where(lane < nh, -jnp.exp(al_ref[...]) * softplus,
                       jnp.where(lane < 2 * nh, jax.nn.sigmoid(ab), 0.0))
    pr = _iota((LANES, 2 * LANES), 0)
    pc = _iota((LANES, 2 * LANES), 1)
    pick = jnp.where(pr == 2 * p + _div64(pc & (LANES - 1)) + jnp.where(pc < LANES, 0, nh), 1.0, 0.0).astype(BF16)
    gbx = _dot_data_01(g_beta, pick)
    gx, bx = gbx[:, :LANES], gbx[:, LANES:]

    sub = min(ts, 2 * LANES)
    tr = _iota((sub, sub), 0)
    tc = _iota((sub, sub), 1)
    in_chunk_tril = jnp.where((_div64(tr) == _div64(tc)) & (tc <= tr), 1.0, 0.0).astype(BF16)
    gcx = jnp.concatenate([_dot_01_data(in_chunk_tril, gx[r0:r0 + sub]) for r0 in range(0, ts, sub)],
                          axis=0)
    nck = ts // GDN_CHUNK
    last = gcx.reshape(nck, GDN_CHUNK, LANES)[:, GDN_CHUNK - 1:GDN_CHUNK, :]
    glx = jnp.broadcast_to(last, (nck, GDN_CHUNK, LANES)).reshape(ts, LANES)
    eg = jnp.exp(gcx)

    kb = kn * bx
    qg_ref[0, 0] = (qn * eg).astype(BF16)
    eg_ref[0, 0] = jnp.exp(glx)
    _gdn_solve_tile(qn.astype(BF16), kn.astype(BF16), kb.astype(BF16), v * bx, kb * eg, kn * jnp.exp(glx - gcx), gcx,
                    x_ref, qk_ref, kdt_ref)


def _gdn_chunks(u, conv_w, a_log, dt_bias, *, q_lane_block, ab_lane_block, ts):
    b, s, _ = u.shape
    qb = q_lane_block
    hb = ts // GDN_HALO
    nh = a_log.shape[0]
    pad = lambda t: jnp.zeros((1, LANES), F32).at[0, :nh].set(t)
    cur = lambda off: pl.BlockSpec((1, ts, LANES), lambda bi, i, p: (bi, i, qb + off + p))
    halo = lambda off: pl.BlockSpec((1, GDN_HALO, LANES), lambda bi, i, p: (bi, jnp.maximum(i * hb - 1, 0), qb + off + p))
    wsp = lambda off: pl.BlockSpec((GDN_CONV_K, LANES), lambda bi, i, p: (0, off + p))
    vec = pl.BlockSpec((1, LANES), lambda bi, i, p: (0, 0))
    one = pl.BlockSpec((1, 1, ts, LANES), lambda bi, i, p: (bi, p, i, 0))
    two = pl.BlockSpec((1, 1, 2 * ts, LANES), lambda bi, i, p: (bi, p, i, 0))
    sh1 = lambda dt: jax.ShapeDtypeStruct((b, 3, s, LANES), dt)
    sh2 = lambda dt: jax.ShapeDtypeStruct((b, 3, 2 * s, LANES), dt)
    return pl.pallas_call(
        functools.partial(_gdn_chunks_kernel, ts=ts),
        grid=(b, s // ts, 3),
        in_specs=[cur(0), cur(3), cur(6), halo(0), halo(3), halo(6),
                  pl.BlockSpec((1, ts, LANES), lambda bi, i, p: (bi, i, ab_lane_block)),
                  wsp(0), wsp(3), wsp(6), vec, vec],
        out_specs=[two, two, two, one, one],
        out_shape=[sh2(F32), sh2(BF16), sh2(BF16), sh1(BF16), sh1(F32)],
        scratch_shapes=[pltpu.VMEM((GDN_HALO + ts, LANES), F32)],
        name="gdn_chunks",
        compiler_params=_cparams(("arbitrary", "arbitrary", "arbitrary")),
    )(u, u, u, u, u, u, u, conv_w, conv_w, conv_w, pad(a_log), pad(dt_bias))


def _stack2(top, bot):
    return jnp.concatenate([top, bot], axis=0)


def _gdn_solve_tile(qn_t, kn_t, kb_t, vb_t, kbg_t, kd_t, gc_t, x_ref, qk_ref, kdt_ref):
    cs = GDN_CHUNK
    n2 = 2 * cs
    chunks = qn_t.shape[0] // cs
    lane = _iota((cs, LANES), 1)
    low = lane < HEAD_DIM
    r = _iota((n2, n2), 0)
    c = _iota((n2, n2), 1)
    bd = _div64(r) == _div64(c)
    eye = jnp.where(r == c, 1.0, 0.0)
    diag8 = (r >> 3) == (c >> 3)
    merge_masks = [((r >> (sh + 1)) == (c >> (sh + 1))) & (((r >> sh) & 1) == 1) & (((c >> sh) & 1) == 0)
                   for sh in (3, 4, 5)]

    def per_head_rows(t, fill=0.0):
        return _stack2(jnp.where(low, t, fill), jnp.where(low, fill, t))

    cis = range(chunks)
    rows = [slice(ci * cs, (ci + 1) * cs) for ci in cis]
    rows2 = [pl.ds(ci * n2, n2) for ci in cis]
    kn = [per_head_rows(kn_t[rw]) for rw in rows]
    kb = [per_head_rows(kb_t[rw]) for rw in rows]
    qn = [per_head_rows(qn_t[rw]) for rw in rows]
    kk = [_dot_nt(kb[ci], kn[ci]) for ci in cis]
    qk = [_dot_nt(qn[ci], kn[ci]) for ci in cis]
    decay = []
    for rw in rows:
        gc = gc_t[rw]
        gsw = pltpu.roll(gc, HEAD_DIM, 1)
        gcol = _stack2(jnp.where(low, gc, gsw), jnp.where(low, gsw, gc))
        decay.append(jnp.exp(jnp.minimum(gcol - gcol.T, 0.0)))
    lmat = [jnp.where(c < r, kk[ci] * decay[ci], 0.0) for ci in cis]
    for ci in cis:
        qk_ref[0, 0, rows2[ci], :] = jnp.where(c <= r, qk[ci] * decay[ci], 0.0).astype(BF16)

    ld = [jnp.where(diag8, lm, 0.0) for lm in lmat]
    ld2 = [_dot_split2(m, m) for m in ld]
    ld4 = [_dot_split2(m, m) for m in ld2]
    t = [eye - m for m in ld]
    t = [t[ci] + _dot_split2(t[ci], ld2[ci]) for ci in cis]
    t = [t[ci] + _dot_split2(t[ci], ld4[ci]) for ci in cis]
    for lower_left in merge_masks:
        tb = [m.astype(BF16) for m in t]
        inner = [_dot(jnp.where(lower_left, lmat[ci], 0.0).astype(BF16), tb[ci]) for ci in cis]
        t = [t[ci] - _dot(tb[ci], inner[ci].astype(BF16)) for ci in cis]
    for ci in cis:
        kbg = kbg_t[rows[ci]]
        vsw = pltpu.roll(vb_t[rows[ci]], HEAD_DIM, 1)
        x = jnp.where(bd, _stack2(kbg, kbg), _stack2(vsw, vsw))
        x_ref[0, 0, rows2[ci], :] = _dot(t[ci].astype(BF16), x.astype(BF16))
    for ci in cis:
        kd = kd_t[rows[ci]]
        kdt_ref[0, 0, rows2[ci], :] = jnp.where(bd, _stack2(kd, kd), 0.0).T.astype(BF16)


def _gdn_scan_kernel(x_ref, qk_ref, kdt_ref, qg_ref, eg_ref, z0_ref, z1_ref, z2_ref, ng_ref, o_ref, st_ref, raw_ref,
                     *, chunks, batch):
    cs = GDN_CHUNK
    n2 = 2 * cs
    r = _iota((n2, n2), 0)
    c = _iota((n2, n2), 1)
    bd = _div64(r) == _div64(c)
    hr = _iota((LANES, LANES), 0)
    hc = _iota((LANES, LANES), 1)
    head_mean = jnp.where(_div64(hr) == _div64(hc), 1.0 / HEAD_DIM, 0.0).astype(BF16)
    z_refs = (z0_ref, z1_ref, z2_ref)

    @pl.when(pl.program_id(0) == 0)
    def _():
        st_ref[...] = jnp.zeros_like(st_ref)

    def body(ci, carry):
        start = pl.multiple_of(ci * cs, cs)
        rows = pl.ds(start, cs)
        rows2 = pl.ds(pl.multiple_of(ci * n2, n2), n2)
        chains = [(bi, p) for bi in range(batch) for p in range(3)]
        ks = range(len(chains))
        st = [st_ref[bi * 3 + p] for bi, p in chains]
        stb = [t.astype(BF16) for t in st]
        x = [x_ref[bi, p, rows2, :] for bi, p in chains]
        ws = [_dot(jnp.where(bd, x[k], 0.0).astype(BF16), stb[k]) for k in ks]
        vnb = [(jnp.where(bd, 0.0, x[k]) - ws[k]).astype(BF16) for k in ks]
        qbd = [jnp.where(bd, _stack2(qg, qg), jnp.zeros((), BF16)) for qg in (qg_ref[bi, p, rows, :] for bi, p in chains)]
        o = [_dot(qbd[k], stb[k]) + _dot(qk_ref[bi, p, rows2, :], vnb[k]) for k, (bi, p) in enumerate(chains)]
        upd = [_dot(kdt_ref[bi, p, rows2, :], vnb[k]) for k, (bi, p) in enumerate(chains)]
        for k, (bi, p) in enumerate(chains):
            eg = pltpu.roll(eg_ref[bi, p, pl.ds(start, 8), :], HEAD_DIM, 1)[0:1, :]
            st_ref[bi * 3 + p] = st[k] * eg + upd[k]
        for k, (bi, p) in enumerate(chains):
            t = o[k]
            raw_ref[bi, p, rows, :] = pltpu.roll(t[:cs] + t[cs:], HEAD_DIM, 1)
        return carry

    lax.fori_loop(0, chunks, body, 0)

    for bi in range(batch):
        for p in range(3):
            op = raw_ref[bi, p]
            ms = _dot_data_01(op * op, head_mean)
            o_ref[bi, p] = (op * lax.rsqrt(ms + EPS) * ng_ref[...] * _silu(z_refs[p][bi])).astype(o_ref.dtype)


def _gdn_scan(x, qk, kdt, qg, eg, u, norm_g, *, z_lane_block, ts):
    b, _, s, _ = qg.shape
    chunks = ts // GDN_CHUNK
    zb = z_lane_block
    two = pl.BlockSpec((b, 3, 2 * ts, LANES), lambda i: (0, 0, i, 0))
    one = pl.BlockSpec((b, 3, ts, LANES), lambda i: (0, 0, i, 0))
    zsp = lambda p: pl.BlockSpec((b, ts, LANES), lambda i: (0, i, zb + p))
    ng = jnp.concatenate([norm_g, norm_g]).reshape(1, LANES)
    return pl.pallas_call(
        functools.partial(_gdn_scan_kernel, chunks=chunks, batch=b),
        grid=(s // ts,),
        in_specs=[two, two, two, one, one, zsp(0), zsp(1), zsp(2), pl.BlockSpec((1, LANES), lambda i: (0, 0))],
        out_specs=one,
        out_shape=jax.ShapeDtypeStruct((b, 3, s, LANES), BF16),
        scratch_shapes=[pltpu.VMEM((b * 3, LANES, LANES), F32),
                        pltpu.VMEM((b, 3, ts, LANES), F32)],
        name="gdn_scan",
        compiler_params=_cparams(("arbitrary",)),
    )(x, qk, kdt, qg, eg, u, u, u, ng)


def _outproj_kernel(x_ref, mod_ref, yc_ref, ym_ref, yg_ref, w_ref, o_ref):
    parts = [yc_ref[0]] + [y_ref[0, p] for y_ref in (ym_ref, yg_ref) for p in range(3)]
    y = jnp.concatenate([t.astype(BF16) for t in parts], axis=-1)
    o_ref[0] = x_ref[0] + mod_ref[0][5:6] * _dot(y, w_ref[...])


def _outproj(x, mod, yc, ym, yg, w, *, tm):
    b, s, d = x.shape
    ch = yc.shape[-1]
    return pl.pallas_call(
        _outproj_kernel,
        grid=(b, s // tm),
        in_specs=[pl.BlockSpec((1, tm, d), lambda bi, i: (bi, i, 0)),
                  pl.BlockSpec((1, N_MOD, d), lambda bi, i: (bi, 0, 0)),
                  pl.BlockSpec((1, tm, ch), lambda bi, i: (bi, i, 0)),
                  pl.BlockSpec((1, 3, tm, LANES), lambda bi, i: (bi, 0, i, 0)),
                  pl.BlockSpec((1, 3, tm, LANES), lambda bi, i: (bi, 0, i, 0)),
                  pl.BlockSpec(w.shape, lambda bi, i: (0, 0), pipeline_mode=pl.Buffered(1))],
        out_specs=pl.BlockSpec((1, tm, d), lambda bi, i: (bi, i, 0)),
        out_shape=jax.ShapeDtypeStruct((b, s, d), F32),
        name="out_proj",
        compiler_params=_cparams(("arbitrary", "arbitrary")),
    )(x, mod, yc, ym, yg, w)


def _token_mixing(x, mod, ln_g, w_in, conv_w, conv_b, conv_ln_g, conv_ln_b, gdn_conv_w, gdn_a_log, gdn_dt_bias,
                  gdn_norm_g, w_out, tables, *, tm, tq_moba, ts_gdn, ts_scan):
    d, d_in = w_in.shape
    n_pad = -(-d_in // LANES) * LANES
    w_in_p = jnp.zeros((d, n_pad), BF16).at[:, :d_in].set(w_in.astype(BF16))
    u, y_conv, qp, ka, va, km = _inproj(x, mod, ln_g, w_in_p, tables, conv_w, conv_b, conv_ln_g, conv_ln_b, tm=tm)

    q_blk_gdn = 0
    z_blk = q_blk_gdn + 9
    ab_blk = z_blk + 3

    y_moba = _moba_attention(qp, ka, va, _block_mean_rows(km), tq=tq_moba)

    xs, qk, kdt, qg, eg = _gdn_chunks(u, gdn_conv_w, gdn_a_log, gdn_dt_bias, q_lane_block=q_blk_gdn,
                                       ab_lane_block=ab_blk, ts=ts_gdn)
    y_gdn = _gdn_scan(xs, qk, kdt, qg, eg, u, gdn_norm_g, z_lane_block=z_blk, ts=ts_scan)

    return _outproj(x, mod, y_conv, y_moba, y_gdn, w_out.astype(BF16), tm=tm)


def kernel(x, c, w_ada, b_ada, ln_ffn1_g, ffn1_w_gate, ffn1_w_up, ffn1_w_down, ln_mix_g, w_in, conv_w, conv_b, conv_ln_g, conv_ln_b, gdn_conv_w, gdn_a_log, gdn_dt_bias, gdn_norm_g, w_out, ln_ffn2_g, ffn2_w_gate, ffn2_w_up, ffn2_w_down, final_g):
    depth = w_ada.shape[0]
    s = x.shape[1]
    tm = min(512, s)
    tiles = dict(tm=tm, tq_moba=min(1024, s), ts_gdn=min(1024, s),
                 ts_scan=min(512, s))
    mods = _ada_mod(c, w_ada, b_ada)
    tables = _rope_tables(s)
    for l in range(depth):
        mod = mods[l]
        x = _ffn(x, mod, ln_ffn1_g, ffn1_w_gate, ffn1_w_up, ffn1_w_down, final_g,
                 layer=l, mod_base=0, final=False, tm=tm)
        x = _token_mixing(x, mod, ln_mix_g[l], w_in[l], conv_w[l], conv_b[l], conv_ln_g[l], conv_ln_b[l],
                          gdn_conv_w[l], gdn_a_log[l], gdn_dt_bias[l], gdn_norm_g[l], w_out[l], tables, **tiles)
        x = _ffn(x, mod, ln_ffn2_g, ffn2_w_gate, ffn2_w_up, ffn2_w_down, final_g,
                 layer=l, mod_base=6, final=(l == depth - 1), tm=tm)
    return x
```

```python
import functools
import math

import numpy as np
import jax
import jax.numpy as jnp
from jax import lax
from jax.experimental import pallas as pl
from jax.experimental.pallas import tpu as pltpu

F32 = jnp.float32
BF16 = jnp.bfloat16

EPS = 1e-6
HEAD_DIM = 64
LANES = 128
SUBLANES = 8
N_MOD = 9
FFN_CHUNK = 256
CONV_K = 31
CONV_HALO = 32
MOBA_BLOCK = 256
MOBA_TOPK = 3
MOBA_GROUP = 4
ROPE_DIM = HEAD_DIM // 4
ROPE_THETA = 500000.0
GDN_CONV_K = 4
GDN_HALO = 8
GDN_CHUNK = 64
MASK_NEG = -(2.0 ** 100)
VMEM_LIMIT = 56 * 1024 * 1024


def _cparams(sem):
    return pltpu.CompilerParams(dimension_semantics=sem, vmem_limit_bytes=VMEM_LIMIT)


def _iota(shape, dim):
    return lax.broadcasted_iota(jnp.int32, shape, dim)


def _div64(t):
    return t >> 6


def _dot(a, b):
    return jnp.dot(a, b, preferred_element_type=F32)


def _dot_nt(a, b):
    return lax.dot_general(a, b, (((1,), (1,)), ((), ())), preferred_element_type=F32)


def _split3(x):
    x0 = x.astype(BF16)
    r1 = x - x0.astype(F32)
    x1 = r1.astype(BF16)
    x2 = (r1 - x1.astype(F32)).astype(BF16)
    return x0, x1, x2


def _dot_nt_split3(a, b):
    a0, a1, a2 = _split3(a)
    b0, b1, b2 = _split3(b)
    small = _dot_nt(a0, b2) + _dot_nt(a1, b1) + _dot_nt(a2, b0)
    mid = _dot_nt(a0, b1) + _dot_nt(a1, b0)
    return _dot_nt(a0, b0) + (mid + small)


def _dot_data_01(x, m01):
    x0, x1, x2 = _split3(x)
    return _dot(x0, m01) + (_dot(x1, m01) + _dot(x2, m01))


def _dot_01_data(m01, x):
    x0, x1, x2 = _split3(x)
    return _dot(m01, x0) + (_dot(m01, x1) + _dot(m01, x2))


def _dot_split2(a, b):
    a0 = a.astype(BF16)
    a1 = (a - a0.astype(F32)).astype(BF16)
    b0 = b.astype(BF16)
    b1 = (b - b0.astype(F32)).astype(BF16)
    return _dot(a0, b0) + (_dot(a0, b1) + _dot(a1, b0))


def _silu(x):
    return x * jax.nn.sigmoid(x)


def _modulate(x, g, shift, scale):
    ms = jnp.mean(x * x, axis=-1, keepdims=True)
    y = x * lax.rsqrt(ms + EPS) * g
    return y * (1.0 + scale) + shift


def _ada_kernel(c_ref, w_ref, b_ref, o_ref):
    ca = _silu(c_ref[...]).astype(BF16)
    o_ref[0] = _dot(ca, w_ref[0].astype(BF16)) + b_ref[0]


def _ada_mod(c, w_ada, b_ada):
    depth, d, nd = w_ada.shape
    b = c.shape[0]
    rows = 8
    c8 = jnp.zeros((rows, d), F32).at[:b].set(c)
    out = pl.pallas_call(
        _ada_kernel,
        grid=(depth, nd // d),
        in_specs=[pl.BlockSpec((rows, d), lambda l, j: (0, 0)),
                  pl.BlockSpec((1, d, d), lambda l, j: (l, 0, j)),
                  pl.BlockSpec((1, 1, d), lambda l, j: (l, 0, j))],
        out_specs=pl.BlockSpec((1, rows, d), lambda l, j: (l, 0, j)),
        out_shape=jax.ShapeDtypeStruct((depth, rows, nd), F32),
        name="ada_mod",
        compiler_params=_cparams(("arbitrary", "arbitrary")),
    )(c8, w_ada, b_ada.reshape(depth, 1, nd))
    return out[:, :b].reshape(depth, b, nd // d, d)


def _ffn_kernel(x_ref, mod_ref, g_ref, wg_ref, wu_ref, wd_ref, fg_ref, o_ref, *, mod_base, final):
    x = x_ref[0]
    m = mod_ref[0]
    shift, scale, gate = m[mod_base:mod_base + 1], m[mod_base + 1:mod_base + 2], m[mod_base + 2:mod_base + 3]
    h = _modulate(x, g_ref[0], shift, scale).astype(BF16)
    down = None
    for c0 in range(0, wg_ref.shape[2], FFN_CHUNK):
        a = _dot(h, wg_ref[0, :, c0:c0 + FFN_CHUNK].astype(BF16))
        u = _dot(h, wu_ref[0, :, c0:c0 + FFN_CHUNK].astype(BF16))
        act = (_silu(a) * u).astype(BF16)
        part = _dot(act, wd_ref[0, c0:c0 + FFN_CHUNK, :].astype(BF16))
        down = part if down is None else down + part
    y = x + (0.5 * gate) * down
    if final:
        y = y * lax.rsqrt(jnp.mean(y * y, axis=-1, keepdims=True) + EPS) * fg_ref[...]
    o_ref[0] = y


def _ffn(x, mod, g, wg, wu, wd, final_g, *, layer, mod_base, final, tm):
    b, s, d = x.shape
    depth, _, f = wg.shape
    resident = dict(pipeline_mode=pl.Buffered(1))
    return pl.pallas_call(
        functools.partial(_ffn_kernel, mod_base=mod_base, final=final),
        grid=(b, s // tm),
        in_specs=[pl.BlockSpec((1, tm, d), lambda bi, i: (bi, i, 0)),
                  pl.BlockSpec((1, N_MOD, d), lambda bi, i: (bi, 0, 0)),
                  pl.BlockSpec((1, 1, d), lambda bi, i: (layer, 0, 0)),
                  pl.BlockSpec((1, d, f), lambda bi, i: (layer, 0, 0), **resident),
                  pl.BlockSpec((1, d, f), lambda bi, i: (layer, 0, 0), **resident),
                  pl.BlockSpec((1, f, d), lambda bi, i: (layer, 0, 0), **resident),
                  pl.BlockSpec((1, d), lambda bi, i: (0, 0))],
        out_specs=pl.BlockSpec((1, tm, d), lambda bi, i: (bi, i, 0)),
        out_shape=jax.ShapeDtypeStruct((b, s, d), F32),
        name="ffn",
        compiler_params=_cparams(("arbitrary", "arbitrary")),
    )(x, mod, g.reshape(depth, 1, d), wg, wu, wd, final_g.reshape(1, d))


def _inproj_kernel(x_ref, mod_ref, g_ref, w_ref, c_ref, sa_ref, sb_ref, cw_ref, cb_ref, clg_ref, clb_ref,
                   o_ref, yc_ref, qo_ref, ko_ref, vo_ref, km_ref, halo_ref, hs_ref, sh_ref, *, moba_lo):
    i = pl.program_id(1)
    m = mod_ref[0]
    h = _modulate(x_ref[0], g_ref[...], m[3:4], m[4:5]).astype(BF16)
    u = _dot(h, w_ref[...])
    moba_hi = moba_lo + 9 * LANES
    o_ref[0] = u[:, moba_hi:]
    yc_ref[0] = _conv_tile(u[:, :moba_lo], i == 0, cw_ref, cb_ref, clg_ref, clb_ref,
                           halo_ref, hs_ref, sh_ref).astype(yc_ref.dtype)
    c, sa, sb = c_ref[...], sa_ref[...], sb_ref[...]
    first_blk = i * (u.shape[0] // MOBA_BLOCK)
    for p in range(3):
        q, k, v = (u[:, moba_lo + (3 * j + p) * LANES:moba_lo + (3 * j + p + 1) * LANES] for j in range(3))
        _moba_prep_pair(q, k, v, c, sa, sb, first_blk, p, qo_ref, ko_ref, vo_ref, km_ref)


def _inproj(x, mod, g, w, tables, conv_w, conv_b, conv_ln_g, conv_ln_b, *, tm):
    b, s, d = x.shape
    n = w.shape[1]
    ch = conv_w.shape[1]
    moba_lo = 2 * ch
    n_rest = n - moba_lo - 9 * LANES
    nb = s // MOBA_BLOCK
    nbs = tm // MOBA_BLOCK
    tab = pl.BlockSpec((tm, LANES), lambda bi, i: (i, 0))
    vec = pl.BlockSpec((1, ch), lambda bi, i: (0, 0))
    hsp = pl.BlockSpec((1, 6, tm, LANES), lambda bi, i: (bi, 0, i, 0))
    head = lambda dt: jax.ShapeDtypeStruct((b, 6, s, LANES), dt)
    return pl.pallas_call(
        functools.partial(_inproj_kernel, moba_lo=moba_lo),
        grid=(b, s // tm),
        in_specs=[pl.BlockSpec((1, tm, d), lambda bi, i: (bi, i, 0)),
                  pl.BlockSpec((1, N_MOD, d), lambda bi, i: (bi, 0, 0)),
                  pl.BlockSpec((1, d), lambda bi, i: (0, 0)),
                  pl.BlockSpec((d, n), lambda bi, i: (0, 0), pipeline_mode=pl.Buffered(1)),
                  tab, tab, tab,
                  pl.BlockSpec((CONV_K, ch), lambda bi, i: (0, 0)), vec, vec, vec],
        out_specs=[pl.BlockSpec((1, tm, n_rest), lambda bi, i: (bi, i, 0)),
                   pl.BlockSpec((1, tm, ch), lambda bi, i: (bi, i, 0)), hsp, hsp,
                   pl.BlockSpec((1, 6, LANES, tm), lambda bi, i: (bi, 0, 0, i)),
                   pl.BlockSpec((1, 6, nbs, 1, LANES), lambda bi, i: (bi, 0, i, 0, 0))],
        out_shape=[jax.ShapeDtypeStruct((b, s, n_rest), F32),
                   jax.ShapeDtypeStruct((b, s, ch), BF16),
                   head(F32), head(BF16),
                   jax.ShapeDtypeStruct((b, 6, LANES, s), BF16),
                   jax.ShapeDtypeStruct((b, 6, nb, 1, LANES), F32)],
        scratch_shapes=[pltpu.VMEM((CONV_HALO, ch), F32),
                        pltpu.VMEM((CONV_HALO + tm + SUBLANES, ch), F32),
                        pltpu.VMEM((SUBLANES, CONV_HALO + tm, ch), F32)],
        name="in_proj",
        compiler_params=_cparams(("arbitrary", "arbitrary")),
    )(x, mod, g.reshape(1, d), w, *tables, conv_w, conv_b.reshape(1, ch), conv_ln_g.reshape(1, ch),
      conv_ln_b.reshape(1, ch))


def _conv_tile(u_conv, first, w_ref, b_ref, lg_ref, lb_ref, halo_ref, hs_ref, sh_ref):
    ts, ch = u_conv.shape[0], u_conv.shape[1] // 2
    h_cur = u_conv[:, :ch] * jax.nn.sigmoid(u_conv[:, ch:])
    hs_ref[0:CONV_HALO, :] = jnp.where(first, 0.0, halo_ref[...])
    hs_ref[CONV_HALO:CONV_HALO + ts, :] = h_cur
    halo_ref[...] = h_cur[ts - CONV_HALO:, :]
    hs_ref[CONV_HALO + ts:CONV_HALO + ts + SUBLANES, :] = jnp.zeros((SUBLANES, ch), F32)
    span = CONV_HALO + ts
    for r in range(SUBLANES):
        sh_ref[r] = hs_ref[pl.ds(r, span), :]
    base = CONV_HALO - (CONV_K - 1)
    acc = jnp.zeros((ts, ch), F32) + b_ref[...]
    for k in range(CONV_K):
        r = (base + k) % SUBLANES
        acc = acc + sh_ref[r, pl.ds(base + k - r, ts), :] * w_ref[k:k + 1, :]
    mu = jnp.mean(acc, axis=-1, keepdims=True)
    cen = acc - mu
    var = jnp.mean(cen * cen, axis=-1, keepdims=True)
    return _silu(cen * lax.rsqrt(var + EPS) * lg_ref[...] + lb_ref[...])


def _rope_tables(s):
    half = ROPE_DIM // 2
    inv = jnp.exp(-math.log(ROPE_THETA) * jnp.arange(0, ROPE_DIM, 2, dtype=F32) / ROPE_DIM)
    ang = jnp.arange(s, dtype=F32)[:, None] * inv[None, :]
    cs = jnp.concatenate([jnp.cos(ang), jnp.sin(ang), jnp.ones((s, 1), F32)], axis=-1)
    d = np.arange(LANES) % HEAD_DIM
    rows = 2 * half + 1
    mc, msa, msb = (np.zeros((rows, LANES), np.float32) for _ in range(3))
    for lane_i, di in enumerate(d):
        if di < ROPE_DIM:
            mc[di % half, lane_i] = 1.0
        else:
            mc[2 * half, lane_i] = 1.0
        if di < half:
            msa[half + di, lane_i] = -1.0
        elif di < ROPE_DIM:
            msb[half + (di - half), lane_i] = 1.0
    spread = lambda m: jnp.dot(cs, jnp.asarray(m), precision=lax.Precision.HIGHEST)
    return spread(mc), spread(msa), spread(msb)


def _mask_base(hh):
    return (1 - hh) * HEAD_DIM


def _moba_prep_pair(q, k, v, c, sa, sb, first_blk, p, qo_ref, ko_ref, vo_ref, km_ref):
    half = ROPE_DIM // 2

    def rope(x):
        return x * c + pltpu.roll(x, LANES - half, 1) * sa + pltpu.roll(x, half, 1) * sb

    qr, kr = rope(q), rope(k)
    lane = _iota(qr.shape, 1)
    nbs = qr.shape[0] // MOBA_BLOCK
    blk_of_row = first_blk + (_iota(qr.shape, 0) >> (MOBA_BLOCK.bit_length() - 1))
    for hh in range(2):
        own = (lane >= hh * HEAD_DIM) & (lane < (hh + 1) * HEAD_DIM)
        mb = _mask_base(hh)
        kz = jnp.where(own, kr, 0.0)
        qo_ref[0, 2 * p + hh] = jnp.where(own, qr, 0.0)
        ko_ref[0, 2 * p + hh] = jnp.where(own, kr, jnp.where(lane == mb + blk_of_row, 1.0, 0.0)).astype(BF16)
        vo_ref[0, 2 * p + hh] = jnp.where(own, v, jnp.where(lane == mb, 1.0, 0.0)).T.astype(BF16)
        for j in range(nbs):
            km_ref[0, 2 * p + hh, j] = jnp.mean(kz[j * MOBA_BLOCK:(j + 1) * MOBA_BLOCK], axis=0, keepdims=True)


def _block_mean_rows(km):
    b, h, nb = km.shape[:3]
    km = km[:, :, :, 0, :].reshape(b, h // 2, 2, nb, LANES)
    rows = [jnp.zeros((b, h // 2, LANES, LANES), F32).at[:, :, _mask_base(hh):_mask_base(hh) + nb, :].set(km[:, :, hh])
            for hh in range(2)]
    return jnp.stack(rows, axis=2).reshape(b, h, LANES, LANES)


def _moba_kernel(q_ref, ka_ref, vat_ref, kmt_ref, o_ref, sa_ref, sb_ref, m_ref, acc_ref, *, tq):
    i = pl.program_id(2)
    blk_shift = MOBA_BLOCK.bit_length() - 1
    gk = MOBA_GROUP * MOBA_BLOCK
    first_blk = i * (tq // MOBA_BLOCK)
    lane = _iota((tq, LANES), 1)
    nsel = LANES // 4
    blk_id = _iota((nsel, tq), 0)
    blk_f = blk_id.astype(F32)
    own_blk = first_blk + (_iota((nsel, tq), 1) >> blk_shift)
    neg_inf = -jnp.inf

    qas = []
    for hh in range(2):
        mb = _mask_base(hh)
        qp = q_ref[0, hh]
        gate = _dot_nt_split3(kmt_ref[0, hh, mb:mb + nsel, :], qp)
        s = jnp.where(blk_id < own_blk, gate, neg_inf)
        sel = jnp.where(blk_id == own_blk, 1.0, 0.0)
        for _ in range(MOBA_TOPK):
            m = jnp.max(s, axis=0, keepdims=True)
            idx = jnp.min(jnp.where(s == m, blk_f, float(LANES)), axis=0, keepdims=True)
            hit = blk_f == idx
            sel = jnp.where(hit, 1.0, sel)
            s = jnp.where(hit, neg_inf, s)
        neg_t = jnp.where(sel > 0.5, 0.0, MASK_NEG)
        pad_lo = jnp.zeros((mb, tq), F32)
        pad_hi = jnp.zeros((LANES - mb - nsel, tq), F32)
        negm_t = jnp.concatenate([t for t in (pad_lo, neg_t, pad_hi) if t.shape[0]], axis=0)
        qas.append((qp.T * (HEAD_DIM ** -0.5 * math.log2(math.e)) + negm_t).astype(BF16))

    def scores(buf, g):
        start = pl.multiple_of(g * gk, gk)
        for hh in range(2):
            buf[hh] = _dot(ka_ref[0, hh, pl.ds(start, gk), :], qas[hh])

    def consume(buf, g, causal):
        start = pl.multiple_of(g * gk, gk)
        if causal and tq == gk:
            return consume_own(buf, start)
        for hh in range(2):
            s = buf[hh]
            if causal:
                s = jnp.where(start + _iota((gk, tq), 0) <= i * tq + _iota((gk, tq), 1), s, MASK_NEG)
            m = m_ref[hh]
            mn = jnp.maximum(m, jnp.max(s, axis=0, keepdims=True))
            p = jnp.exp2(s - mn).astype(BF16)
            pv = _dot(vat_ref[0, hh, :, pl.ds(start, gk)], p)
            m_ref[hh] = mn
            acc_ref[hh] = jnp.exp2(m - mn) * acc_ref[hh] + pv

    def consume_own(buf, start):
        hk = gk // 2
        tri = _iota((hk, hk), 0) <= _iota((hk, hk), 1)
        for hh in range(2):
            top = buf[hh, 0:hk, :]
            top = jnp.concatenate([jnp.where(tri, top[:, :hk], MASK_NEG), top[:, hk:]], axis=1)
            bot = jnp.where(tri, buf[hh, hk:gk, hk:tq], MASK_NEG)
            m = m_ref[hh]
            mn = jnp.maximum(m, jnp.max(top, axis=0, keepdims=True))
            mn = jnp.concatenate([mn[:, :hk], jnp.maximum(mn[:, hk:], jnp.max(bot, axis=0, keepdims=True))], axis=1)
            p_top = jnp.exp2(top - mn).astype(BF16)
            p_bot = jnp.exp2(bot - mn[:, hk:]).astype(BF16)
            pv = _dot(vat_ref[0, hh, :, pl.ds(start, hk)], p_top)
            pv_hi = _dot(vat_ref[0, hh, :, pl.ds(pl.multiple_of(start + hk, hk), hk)], p_bot)
            pv = jnp.concatenate([pv[:, :hk], pv[:, hk:] + pv_hi], axis=1)
            m_ref[hh] = mn
            acc_ref[hh] = jnp.exp2(m - mn) * acc_ref[hh] + pv

    m_ref[...] = jnp.full(m_ref.shape, neg_inf, F32)
    acc_ref[...] = jnp.zeros(acc_ref.shape, F32)
    n_past = first_blk >> (MOBA_GROUP.bit_length() - 1)
    scores(sa_ref, 0)

    def body(j, carry):
        g = 2 * j
        scores(sb_ref, g + 1)
        consume(sa_ref, g, False)
        scores(sa_ref, g + 2)
        consume(sb_ref, g + 1, False)
        return carry

    lax.fori_loop(0, n_past >> 1, body, 0)
    odd = (n_past & 1) == 1

    @pl.when(odd)
    def _():
        scores(sb_ref, n_past)
        consume(sa_ref, n_past - 1, False)
        consume(sb_ref, n_past, True)

    @pl.when(jnp.logical_not(odd))
    def _():
        consume(sa_ref, n_past, True)

    l0, l1 = _mask_base(0), _mask_base(1)
    acc0, acc1 = acc_ref[0], acc_ref[1]
    o0 = (acc0 / acc0[l0:l0 + 1, :]).T
    o1 = (acc1 / acc1[l1:l1 + 1, :]).T
    o_ref[0, 0] = jnp.where(lane < HEAD_DIM, o0, o1).astype(o_ref.dtype)


def _moba_attention(qp, ka, vat, kmt, *, tq):
    b, _, s, _ = qp.shape
    assert s % (MOBA_GROUP * MOBA_BLOCK) == 0 and (MOBA_GROUP * MOBA_BLOCK) % tq == 0 and tq % MOBA_BLOCK == 0
    assert s // MOBA_BLOCK <= LANES // 4
    blk = tq
    gk = MOBA_GROUP * MOBA_BLOCK
    return pl.pallas_call(
        functools.partial(_moba_kernel, tq=tq),
        grid=(b, 3, s // tq),
        in_specs=[pl.BlockSpec((1, 2, blk, LANES), lambda bi, p, i: (bi, p, i, 0)),
                  pl.BlockSpec((1, 2, s, LANES), lambda bi, p, i: (bi, p, 0, 0)),
                  pl.BlockSpec((1, 2, LANES, s), lambda bi, p, i: (bi, p, 0, 0)),
                  pl.BlockSpec((1, 2, LANES, LANES), lambda bi, p, i: (bi, p, 0, 0))],
        out_specs=pl.BlockSpec((1, 1, blk, LANES), lambda bi, p, i: (bi, p, i, 0)),
        out_shape=jax.ShapeDtypeStruct((b, 3, s, LANES), BF16),
        scratch_shapes=[pltpu.VMEM((2, gk, tq), F32), pltpu.VMEM((2, gk, tq), F32),
                        pltpu.VMEM((2, 1, tq), F32), pltpu.VMEM((2, LANES, tq), F32)],
        name="moba_attention",
        compiler_params=_cparams(("arbitrary", "arbitrary", "arbitrary")),
    )(qp, ka, vat, kmt)


def _gdn_chunks_kernel(q_ref, k_ref, v_ref, hq_ref, hk_ref, hv_ref, ab_ref, wq_ref, wk_ref, wv_ref, al_ref, dt_ref,
                       x_ref, qk_ref, kdt_ref, qg_ref, eg_ref, xs_ref, *, ts):
    i = pl.program_id(1)
    p = pl.program_id(2)
    base = GDN_HALO - (GDN_CONV_K - 1)

    def conv_silu(x_ref, h_ref, w_ref):
        xs_ref[0:GDN_HALO, :] = jnp.where(i > 0, h_ref[0], 0.0)
        xs_ref[GDN_HALO:GDN_HALO + ts, :] = x_ref[0]
        acc = jnp.zeros((ts, LANES), F32)
        for k in range(GDN_CONV_K):
            acc = acc + xs_ref[pl.ds(base + k, ts), :] * w_ref[k:k + 1, :]
        return _silu(acc)

    q, k, v = conv_silu(q_ref, hq_ref, wq_ref), conv_silu(k_ref, hk_ref, wk_ref), conv_silu(v_ref, hv_ref, wv_ref)

    r2 = _iota((2 * LANES, 2 * LANES), 0)
    c2 = _iota((2 * LANES, 2 * LANES), 1)
    head_sum = jnp.where(_div64(r2) == _div64(c2), 1.0, 0.0).astype(BF16)
    ssq = _dot_data_01(jnp.concatenate([q * q, k * k], axis=1), head_sum)
    qn = q * lax.rsqrt(ssq[:, :LANES] + EPS) * (HEAD_DIM ** -0.5)
    kn = k * lax.rsqrt(ssq[:, LANES:] + EPS)

    ab = ab_ref[0]
    lane = _iota(ab.shape, 1)
    nh = 6
    xa = ab + dt_ref[...]
    softplus = jnp.maximum(xa, 0.0) + jnp.log1p(jnp.exp(-jnp.abs(xa)))
    g_beta = jnp.where(lane < nh, -jnp.exp(al_ref[...]) * softplus,
                       jnp.where(lane < 2 * nh, jax.nn.sigmoid(ab), 0.0))
    pr = _iota((LANES, 2 * LANES), 0)
    pc = _iota((LANES, 2 * LANES), 1)
    pick = jnp.where(pr == 2 * p + _div64(pc & (LANES - 1)) + jnp.where(pc < LANES, 0, nh), 1.0, 0.0).astype(BF16)
    gbx = _dot_data_01(g_beta, pick)
    gx, bx = gbx[:, :LANES], gbx[:, LANES:]

    sub = min(ts, 2 * LANES)
    tr = _iota((sub, sub), 0)
    tc = _iota((sub, sub), 1)
    in_chunk_tril = jnp.where((_div64(tr) == _div64(tc)) & (tc <= tr), 1.0, 0.0).astype(BF16)
    gcx = jnp.concatenate([_dot_01_data(in_chunk_tril, gx[r0:r0 + sub]) for r0 in range(0, ts, sub)],
                          axis=0)
    nck = ts // GDN_CHUNK
    last = gcx.reshape(nck, GDN_CHUNK, LANES)[:, GDN_CHUNK - 1:GDN_CHUNK, :]
    glx = jnp.broadcast_to(last, (nck, GDN_CHUNK, LANES)).reshape(ts, LANES)
    eg = jnp.exp(gcx)

    kb = kn * bx
    qg_ref[0, 0] = (qn * eg).astype(BF16)
    eg_ref[0, 0] = jnp.exp(glx)
    _gdn_solve_tile(qn.astype(BF16), kn.astype(BF16), kb.astype(BF16), v * bx, kb * eg, kn * jnp.exp(glx - gcx), gcx,
                    x_ref, qk_ref, kdt_ref)


def _gdn_chunks(u, conv_w, a_log, dt_bias, *, q_lane_block, ab_lane_block, ts):
    b, s, _ = u.shape
    qb = q_lane_block
    hb = ts // GDN_HALO
    nh = a_log.shape[0]
    pad = lambda t: jnp.zeros((1, LANES), F32).at[0, :nh].set(t)
    cur = lambda off: pl.BlockSpec((1, ts, LANES), lambda bi, i, p: (bi, i, qb + off + p))
    halo = lambda off: pl.BlockSpec((1, GDN_HALO, LANES), lambda bi, i, p: (bi, jnp.maximum(i * hb - 1, 0), qb + off + p))
    wsp = lambda off: pl.BlockSpec((GDN_CONV_K, LANES), lambda bi, i, p: (0, off + p))
    vec = pl.BlockSpec((1, LANES), lambda bi, i, p: (0, 0))
    one = pl.BlockSpec((1, 1, ts, LANES), lambda bi, i, p: (bi, p, i, 0))
    two = pl.BlockSpec((1, 1, 2 * ts, LANES), lambda bi, i, p: (bi, p, i, 0))
    sh1 = lambda dt: jax.ShapeDtypeStruct((b, 3, s, LANES), dt)
    sh2 = lambda dt: jax.ShapeDtypeStruct((b, 3, 2 * s, LANES), dt)
    return pl.pallas_call(
        functools.partial(_gdn_chunks_kernel, ts=ts),
        grid=(b, s // ts, 3),
        in_specs=[cur(0), cur(3), cur(6), halo(0), halo(3), halo(6),
                  pl.BlockSpec((1, ts, LANES), lambda bi, i, p: (bi, i, ab_lane_block)),
                  wsp(0), wsp(3), wsp(6), vec, vec],
        out_specs=[two, two, two, one, one],
        out_shape=[sh2(F32), sh2(BF16), sh2(BF16), sh1(BF16), sh1(F32)],
        scratch_shapes=[pltpu.VMEM((GDN_HALO + ts, LANES), F32)],
        name="gdn_chunks",
        compiler_params=_cparams(("arbitrary", "arbitrary", "arbitrary")),
    )(u, u, u, u, u, u, u, conv_w, conv_w, conv_w, pad(a_log), pad(dt_bias))


def _stack2(top, bot):
    return jnp.concatenate([top, bot], axis=0)


def _gdn_solve_tile(qn_t, kn_t, kb_t, vb_t, kbg_t, kd_t, gc_t, x_ref, qk_ref, kdt_ref):
    cs = GDN_CHUNK
    n2 = 2 * cs
    chunks = qn_t.shape[0] // cs
    lane = _iota((cs, LANES), 1)
    low = lane < HEAD_DIM
    r = _iota((n2, n2), 0)
    c = _iota((n2, n2), 1)
    bd = _div64(r) == _div64(c)
    eye = jnp.where(r == c, 1.0, 0.0)
    diag8 = (r >> 3) == (c >> 3)
    merge_masks = [((r >> (sh + 1)) == (c >> (sh + 1))) & (((r >> sh) & 1) == 1) & (((c >> sh) & 1) == 0)
                   for sh in (3, 4, 5)]

    def per_head_rows(t, fill=0.0):
        return _stack2(jnp.where(low, t, fill), jnp.where(low, fill, t))

    cis = range(chunks)
    rows = [slice(ci * cs, (ci + 1) * cs) for ci in cis]
    rows2 = [pl.ds(ci * n2, n2) for ci in cis]
    kn = [per_head_rows(kn_t[rw]) for rw in rows]
    kb = [per_head_rows(kb_t[rw]) for rw in rows]
    qn = [per_head_rows(qn_t[rw]) for rw in rows]
    kk = [_dot_nt(kb[ci], kn[ci]) for ci in cis]
    qk = [_dot_nt(qn[ci], kn[ci]) for ci in cis]
    decay = []
    for rw in rows:
        gc = gc_t[rw]
        gsw = pltpu.roll(gc, HEAD_DIM, 1)
        gcol = _stack2(jnp.where(low, gc, gsw), jnp.where(low, gsw, gc))
        decay.append(jnp.exp(jnp.minimum(gcol - gcol.T, 0.0)))
    lmat = [jnp.where(c < r, kk[ci] * decay[ci], 0.0) for ci in cis]
    for ci in cis:
        qk_ref[0, 0, rows2[ci], :] = jnp.where(c <= r, qk[ci] * decay[ci], 0.0).astype(BF16)

    ld = [jnp.where(diag8, lm, 0.0) for lm in lmat]
    ld2 = [_dot_split2(m, m) for m in ld]
    ld2b = [m.astype(BF16) for m in ld2]
    ld4 = [_dot(m, m) for m in ld2b]
    t = [eye - m for m in ld]
    t = [t[ci] + _dot_split2(t[ci], ld2[ci]) for ci in cis]
    t = [t[ci] + _dot(t[ci].astype(BF16), ld4[ci].astype(BF16)) for ci in cis]
    for lower_left in merge_masks:
        tb = [m.astype(BF16) for m in t]
        inner = [_dot(jnp.where(lower_left, lmat[ci], 0.0).astype(BF16), tb[ci]) for ci in cis]
        t = [t[ci] - _dot(tb[ci], inner[ci].astype(BF16)) for ci in cis]
    for ci in cis:
        kbg = kbg_t[rows[ci]]
        vsw = pltpu.roll(vb_t[rows[ci]], HEAD_DIM, 1)
        x = jnp.where(bd, _stack2(kbg, kbg), _stack2(vsw, vsw))
        x_ref[0, 0, rows2[ci], :] = _dot(t[ci].astype(BF16), x.astype(BF16))
    for ci in cis:
        kd = kd_t[rows[ci]]
        kdt_ref[0, 0, rows2[ci], :] = jnp.where(bd, _stack2(kd, kd), 0.0).T.astype(BF16)


def _gdn_scan_kernel(x_ref, qk_ref, kdt_ref, qg_ref, eg_ref, z0_ref, z1_ref, z2_ref, ng_ref, o_ref, st_ref, raw_ref,
                     *, chunks, batch):
    cs = GDN_CHUNK
    n2 = 2 * cs
    r = _iota((n2, n2), 0)
    c = _iota((n2, n2), 1)
    bd = _div64(r) == _div64(c)
    hr = _iota((LANES, LANES), 0)
    hc = _iota((LANES, LANES), 1)
    head_mean = jnp.where(_div64(hr) == _div64(hc), 1.0 / HEAD_DIM, 0.0).astype(BF16)
    z_refs = (z0_ref, z1_ref, z2_ref)

    @pl.when(pl.program_id(0) == 0)
    def _():
        st_ref[...] = jnp.zeros_like(st_ref)

    def body(ci, carry):
        start = pl.multiple_of(ci * cs, cs)
        rows = pl.ds(start, cs)
        rows2 = pl.ds(pl.multiple_of(ci * n2, n2), n2)
        chains = [(bi, p) for bi in range(batch) for p in range(3)]
        ks = range(len(chains))
        st = [st_ref[bi * 3 + p] for bi, p in chains]
        stb = [t.astype(BF16) for t in st]
        x = [x_ref[bi, p, rows2, :] for bi, p in chains]
        ws = [_dot(jnp.where(bd, x[k], 0.0).astype(BF16), stb[k]) for k in ks]
        vnb = [(jnp.where(bd, 0.0, x[k]) - ws[k]).astype(BF16) for k in ks]
        qbd = [jnp.where(bd, _stack2(qg, qg), jnp.zeros((), BF16)) for qg in (qg_ref[bi, p, rows, :] for bi, p in chains)]
        o = [_dot(qbd[k], stb[k]) + _dot(qk_ref[bi, p, rows2, :], vnb[k]) for k, (bi, p) in enumerate(chains)]
        upd = [_dot(kdt_ref[bi, p, rows2, :], vnb[k]) for k, (bi, p) in enumerate(chains)]
        for k, (bi, p) in enumerate(chains):
            eg = pltpu.roll(eg_ref[bi, p, pl.ds(start, 8), :], HEAD_DIM, 1)[0:1, :]
            st_ref[bi * 3 + p] = st[k] * eg + upd[k]
        for k, (bi, p) in enumerate(chains):
            t = o[k]
            raw_ref[bi, p, rows, :] = pltpu.roll(t[:cs] + t[cs:], HEAD_DIM, 1)
        return carry

    lax.fori_loop(0, chunks, body, 0)

    for bi in range(batch):
        for p in range(3):
            op = raw_ref[bi, p]
            ms = _dot_data_01(op * op, head_mean)
            o_ref[bi, p] = (op * lax.rsqrt(ms + EPS) * ng_ref[...] * _silu(z_refs[p][bi])).astype(o_ref.dtype)


def _gdn_scan(x, qk, kdt, qg, eg, u, norm_g, *, z_lane_block, ts):
    b, _, s, _ = qg.shape
    chunks = ts // GDN_CHUNK
    zb = z_lane_block
    two = pl.BlockSpec((b, 3, 2 * ts, LANES), lambda i: (0, 0, i, 0))
    one = pl.BlockSpec((b, 3, ts, LANES), lambda i: (0, 0, i, 0))
    zsp = lambda p: pl.BlockSpec((b, ts, LANES), lambda i: (0, i, zb + p))
    ng = jnp.concatenate([norm_g, norm_g]).reshape(1, LANES)
    return pl.pallas_call(
        functools.partial(_gdn_scan_kernel, chunks=chunks, batch=b),
        grid=(s // ts,),
        in_specs=[two, two, two, one, one, zsp(0), zsp(1), zsp(2), pl.BlockSpec((1, LANES), lambda i: (0, 0))],
        out_specs=one,
        out_shape=jax.ShapeDtypeStruct((b, 3, s, LANES), BF16),
        scratch_shapes=[pltpu.VMEM((b * 3, LANES, LANES), F32),
                        pltpu.VMEM((b, 3, ts, LANES), F32)],
        name="gdn_scan",
        compiler_params=_cparams(("arbitrary",)),
    )(x, qk, kdt, qg, eg, u, u, u, ng)


def _outproj_kernel(x_ref, mod_ref, yc_ref, ym_ref, yg_ref, w_ref, o_ref):
    parts = [yc_ref[0]] + [y_ref[0, p] for y_ref in (ym_ref, yg_ref) for p in range(3)]
    y = jnp.concatenate([t.astype(BF16) for t in parts], axis=-1)
    o_ref[0] = x_ref[0] + mod_ref[0][5:6] * _dot(y, w_ref[...])


def _outproj(x, mod, yc, ym, yg, w, *, tm):
    b, s, d = x.shape
    ch = yc.shape[-1]
    return pl.pallas_call(
        _outproj_kernel,
        grid=(b, s // tm),
        in_specs=[pl.BlockSpec((1, tm, d), lambda bi, i: (bi, i, 0)),
                  pl.BlockSpec((1, N_MOD, d), lambda bi, i: (bi, 0, 0)),
                  pl.BlockSpec((1, tm, ch), lambda bi, i: (bi, i, 0)),
                  pl.BlockSpec((1, 3, tm, LANES), lambda bi, i: (bi, 0, i, 0)),
                  pl.BlockSpec((1, 3, tm, LANES), lambda bi, i: (bi, 0, i, 0)),
                  pl.BlockSpec(w.shape, lambda bi, i: (0, 0), pipeline_mode=pl.Buffered(1))],
        out_specs=pl.BlockSpec((1, tm, d), lambda bi, i: (bi, i, 0)),
        out_shape=jax.ShapeDtypeStruct((b, s, d), F32),
        name="out_proj",
        compiler_params=_cparams(("arbitrary", "arbitrary")),
    )(x, mod, yc, ym, yg, w)


def _token_mixing(x, mod, ln_g, w_in, conv_w, conv_b, conv_ln_g, conv_ln_b, gdn_conv_w, gdn_a_log, gdn_dt_bias,
                  gdn_norm_g, w_out, tables, *, tm, tm_out, tq_moba, ts_gdn, ts_scan):
    d, d_in = w_in.shape
    n_pad = -(-d_in // LANES) * LANES
    w_in_p = jnp.zeros((d, n_pad), BF16).at[:, :d_in].set(w_in.astype(BF16))
    u, y_conv, qp, ka, va, km = _inproj(x, mod, ln_g, w_in_p, tables, conv_w, conv_b, conv_ln_g, conv_ln_b, tm=tm)

    q_blk_gdn = 0
    z_blk = q_blk_gdn + 9
    ab_blk = z_blk + 3

    y_moba = _moba_attention(qp, ka, va, _block_mean_rows(km), tq=tq_moba)

    xs, qk, kdt, qg, eg = _gdn_chunks(u, gdn_conv_w, gdn_a_log, gdn_dt_bias, q_lane_block=q_blk_gdn,
                                       ab_lane_block=ab_blk, ts=ts_gdn)
    y_gdn = _gdn_scan(xs, qk, kdt, qg, eg, u, gdn_norm_g, z_lane_block=z_blk, ts=ts_scan)

    return _outproj(x, mod, y_conv, y_moba, y_gdn, w_out.astype(BF16), tm=tm_out)


def kernel(x, c, w_ada, b_ada, ln_ffn1_g, ffn1_w_gate, ffn1_w_up, ffn1_w_down, ln_mix_g, w_in, conv_w, conv_b, conv_ln_g, conv_ln_b, gdn_conv_w, gdn_a_log, gdn_dt_bias, gdn_norm_g, w_out, ln_ffn2_g, ffn2_w_gate, ffn2_w_up, ffn2_w_down, final_g):
    depth = w_ada.shape[0]
    s = x.shape[1]
    tm = min(512, s)
    tiles = dict(tm=tm, tm_out=min(1024, s), tq_moba=min(1024, s), ts_gdn=min(1024, s), ts_scan=min(512, s))
    mods = _ada_mod(c, w_ada, b_ada)
    tables = _rope_tables(s)
    for l in range(depth):
        mod = mods[l]
        x = _ffn(x, mod, ln_ffn1_g, ffn1_w_gate, ffn1_w_up, ffn1_w_down, final_g,
                 layer=l, mod_base=0, final=False, tm=tm)
        x = _token_mixing(x, mod, ln_mix_g[l], w_in[l], conv_w[l], conv_b[l], conv_ln_g[l], conv_ln_b[l],
                          gdn_conv_w[l], gdn_a_log[l], gdn_dt_bias[l], gdn_norm_g[l], w_out[l], tables, **tiles)
        x = _ffn(x, mod, ln_ffn2_g, ffn2_w_gate, ffn2_w_up, ffn2_w_down, final_g,
                 layer=l, mod_base=6, final=(l == depth - 1), tm=tm)
    return x
```
